```python
import math
import jax, jax.numpy as jnp
from jax import lax
import numpy as np

D_MODEL = 1024
BATCH = 16
SEQ = 2048
DEPTH = 1

ATTN_HEADS = 8
ATTN_KV_HEADS = 2
HEAD_DIM = 64
ATTN_WIDTH = ATTN_HEADS * HEAD_DIM
KV_WIDTH = ATTN_KV_HEADS * HEAD_DIM
IDX_HEADS = 8
IDX_DIM = 64
IDX_TOPK_MAX = 256
Q_BLOCK = 128
SSD_HEADS = 8
SSD_HEAD_DIM = 64
SSD_WIDTH = SSD_HEADS * SSD_HEAD_DIM
SSD_GROUPS = 2
SSD_STATE = 128
SSD_CONV = 4
SSD_CHUNK = 128
CONV_DIM = SSD_WIDTH + 2 * SSD_GROUPS * SSD_STATE
MIX_WIDTH = ATTN_WIDTH + SSD_WIDTH
IN_SIZES = (ATTN_WIDTH, KV_WIDTH, KV_WIDTH, IDX_HEADS * IDX_DIM, IDX_DIM, IDX_HEADS, SSD_WIDTH, CONV_DIM, SSD_HEADS)
D_IN_PROJ = ATTN_WIDTH + 2 * KV_WIDTH + IDX_HEADS * IDX_DIM + IDX_DIM + IDX_HEADS + SSD_WIDTH + CONV_DIM + SSD_HEADS
PEER_HEADS = 8
PEER_NKEYS = 128
PEER_NEXPERTS = PEER_NKEYS * PEER_NKEYS
PEER_KEY_DIM = 128
PEER_HALF = PEER_KEY_DIM // 2
PEER_TOPK = 16
PEER_TOKEN_BLOCK = 128
ROPE_THETA = 10000.0
NORM_EPS = 1e-6

kernel_name = 'hybrid_dsa_ssd_peer_block'


def rmsnorm(x, g):
    xf = x.astype(jnp.float32)
    y = xf * lax.rsqrt(jnp.mean(xf * xf, axis=-1, keepdims=True) + NORM_EPS)
    return (y * g.astype(jnp.float32)).astype(x.dtype)


def rope_tables(positions, dim):
    inv_freq = ROPE_THETA ** (-jnp.arange(0, dim, 2, dtype=jnp.float32) / dim)
    ang = positions.astype(jnp.float32)[..., None] * inv_freq
    return jnp.cos(ang)[:, :, None, :], jnp.sin(ang)[:, :, None, :]


def apply_rope(x, cos, sin):
    half = x.shape[-1] // 2
    x1 = x[..., :half].astype(jnp.float32)
    x2 = x[..., half:].astype(jnp.float32)
    out = jnp.concatenate([x1 * cos - x2 * sin, x2 * cos + x1 * sin], axis=-1)
    return out.astype(x.dtype)


def dsa_attention(q, k, v, q_idx, k_idx, w_idx):
    b_, s_ = q.shape[0], q.shape[1]
    topk = min(IDX_TOPK_MAX, s_ // 4)
    n_blk = s_ // Q_BLOCK
    grp = ATTN_HEADS // ATTN_KV_HEADS
    key_pos = jnp.arange(s_)
    w_scaled = w_idx.astype(jnp.float32) * (IDX_HEADS ** -0.5)

    def block(i):
        start = i * Q_BLOCK
        qb = lax.dynamic_slice_in_dim(q, start, Q_BLOCK, axis=1)
        qib = lax.dynamic_slice_in_dim(q_idx, start, Q_BLOCK, axis=1)
        wib = lax.dynamic_slice_in_dim(w_scaled, start, Q_BLOCK, axis=1)
        qpos = start + jnp.arange(Q_BLOCK)
        causal = key_pos[None, :] <= qpos[:, None]
        logits = jnp.einsum('bqhd,bsd->bqhs', qib, k_idx).astype(jnp.float32) * (IDX_DIM ** -0.5)
        score = jnp.einsum('bqh,bqhs->bqs', wib, jax.nn.relu(logits))
        score = jnp.where(causal[None], score, -jnp.inf)
        _, sel = lax.top_k(score, topk)
        valid = sel <= qpos[None, :, None]
        k_sel = jax.vmap(lambda kb, ib: kb[ib])(k, sel)
        v_sel = jax.vmap(lambda vb, ib: vb[ib])(v, sel)
        qg = qb.reshape(b_, Q_BLOCK, ATTN_KV_HEADS, grp, HEAD_DIM)
        s = jnp.einsum('bqhgd,bqkhd->bqhgk', qg, k_sel).astype(jnp.float32) * (HEAD_DIM ** -0.5)
        s = jnp.where(valid[:, :, None, None, :], s, -jnp.inf)
        p = jax.nn.softmax(s, axis=-1).astype(v.dtype)
        o = jnp.einsum('bqhgk,bqkhd->bqhgd', p, v_sel)
        return o.reshape(b_, Q_BLOCK, ATTN_WIDTH)

    out = lax.map(block, jnp.arange(n_blk))
    return out.transpose(1, 0, 2, 3).reshape(b_, s_, ATTN_WIDTH)


def segsum(a):
    t = a.shape[-1]
    ii = jnp.arange(t)[:, None]
    jj = jnp.arange(t)[None, :]
    ar = jnp.broadcast_to(a[..., :, None], a.shape + (t,))
    ar = jnp.where(ii > jj, ar, 0.0)
    cs = jnp.cumsum(ar, axis=-2)
    return jnp.where(ii >= jj, cs, -jnp.inf)


def ssd_mixer(z, xbc, dt_raw, conv_w, conv_b, dt_bias, a_log, d_skip, norm_g):
    b_, s_ = z.shape[0], z.shape[1]
    f32 = jnp.float32
    xbc = lax.conv_general_dilated(xbc, conv_w[:, None, :].astype(xbc.dtype), window_strides=(1,),
                                   padding=[(SSD_CONV - 1, 0)], dimension_numbers=('NWC', 'WIO', 'NWC'),
                                   feature_group_count=CONV_DIM)
    xbc = jax.nn.silu(xbc + conv_b.astype(xbc.dtype))
    xs = xbc[..., :SSD_WIDTH]
    bm = xbc[..., SSD_WIDTH:SSD_WIDTH + SSD_GROUPS * SSD_STATE]
    cm = xbc[..., SSD_WIDTH + SSD_GROUPS * SSD_STATE:]
    dt = jax.nn.softplus(dt_raw.astype(f32) + dt_bias.astype(f32))
    a = -jnp.exp(a_log.astype(f32))
    nc = s_ // SSD_CHUNK
    r = SSD_HEADS // SSD_GROUPS
    x = xs.astype(f32).reshape(b_, nc, SSD_CHUNK, SSD_GROUPS, r, SSD_HEAD_DIM)
    dtc = dt.reshape(b_, nc, SSD_CHUNK, SSD_GROUPS, r)
    xdt = x * dtc[..., None]
    bc = bm.astype(f32).reshape(b_, nc, SSD_CHUNK, SSD_GROUPS, SSD_STATE)
    cc = cm.astype(f32).reshape(b_, nc, SSD_CHUNK, SSD_GROUPS, SSD_STATE)
    adt = (dtc * a.reshape(SSD_GROUPS, r)).transpose(0, 3, 4, 1, 2)
    a_cum = jnp.cumsum(adt, axis=-1)
    lmat = jnp.exp(segsum(adt))
    cb = jnp.einsum('bclgn,bcsgn->bgcls', cc, bc)
    y_diag = jnp.einsum('bgcls,bgrcls,bcsgrp->bclgrp', cb, lmat, xdt)
    decay_states = jnp.exp(a_cum[..., -1:] - a_cum)
    states = jnp.einsum('bclgn,bgrcl,bclgrp->bcgrpn', bc, decay_states, xdt)
    states = jnp.concatenate([jnp.zeros_like(states[:, :1]), states], axis=1)
    chunk_decay = jnp.exp(segsum(jnp.pad(a_cum[..., -1], ((0, 0), (0, 0), (0, 0), (1, 0)))))
    states = jnp.einsum('bgrzc,bcgrpn->bzgrpn', chunk_decay, states)[:, :-1]
    y_off = jnp.einsum('bclgn,bcgrpn,bgrcl->bclgrp', cc, states, jnp.exp(a_cum))
    y = y_diag + y_off + x * d_skip.astype(f32).reshape(SSD_GROUPS, r)[..., None]
    y = y.reshape(b_, s_, SSD_WIDTH)
    y = rmsnorm(y * jax.nn.silu(z.astype(f32)), norm_g)
    return y.astype(z.dtype)


def peer_ffn(h, w_q, sub_keys, u, v):
    b_, s_, d_ = h.shape
    ht = h.reshape((b_ * s_) // PEER_TOKEN_BLOCK, PEER_TOKEN_BLOCK, d_)

    def block(xb):
        q = (xb @ w_q).reshape(PEER_TOKEN_BLOCK, PEER_HEADS, 2, PEER_HALF)
        s = jnp.einsum('thid,hikd->thik', q, sub_keys).astype(jnp.float32)
        sv, si = lax.top_k(s, PEER_TOPK)
        cand = (sv[..., 0, :, None] + sv[..., 1, None, :]).reshape(PEER_TOKEN_BLOCK, PEER_HEADS, PEER_TOPK * PEER_TOPK)
        cidx = (si[..., 0, :, None] * PEER_NKEYS + si[..., 1, None, :]).reshape(PEER_TOKEN_BLOCK, PEER_HEADS, PEER_TOPK * PEER_TOPK)
        fv, fpos = lax.top_k(cand, PEER_TOPK)
        eidx = jnp.take_along_axis(cidx, fpos, axis=-1)
        g = jax.nn.softmax(fv, axis=-1)
        u_sel = jnp.take(u, eidx, axis=0)
        act = jax.nn.gelu(jnp.einsum('thkd,td->thk', u_sel, xb).astype(jnp.float32), approximate=False) * g
        v_sel = jnp.take(v, eidx, axis=0)
        return jnp.einsum('thk,thkd->td', act.astype(v.dtype), v_sel)

    out = lax.map(block, ht)
    return out.reshape(b_, s_, d_)


def setup_inputs(seed: int = 0) -> dict:
    key = jax.random.key(seed)
    ks = jax.random.split(key, 24)
    f32 = jnp.float32

    def nrm(k, shape, s):
        return jax.random.normal(k, shape, f32) * s

    x = nrm(ks[0], (BATCH, SEQ, D_MODEL), 1.0)
    c = nrm(ks[1], (BATCH, D_MODEL), 1.0)
    offs = jax.random.randint(ks[2], (BATCH, 1), 0, 64, dtype=jnp.int32)
    positions = offs + jnp.arange(SEQ, dtype=jnp.int32)[None, :]
    w_ada = nrm(ks[3], (DEPTH, D_MODEL, 6 * D_MODEL), 0.5 * D_MODEL ** -0.5)
    b_ada = nrm(ks[4], (DEPTH, 6 * D_MODEL), 0.02)
    pre_mix_g = 1.0 + nrm(ks[5], (DEPTH, D_MODEL), 0.02)
    post_mix_g = 1.0 + nrm(ks[6], (DEPTH, D_MODEL), 0.02)
    w_in = nrm(ks[7], (DEPTH, D_MODEL, D_IN_PROJ), D_MODEL ** -0.5)
    conv_w = nrm(ks[8], (DEPTH, SSD_CONV, CONV_DIM), SSD_CONV ** -0.5)
    conv_b = nrm(ks[9], (DEPTH, CONV_DIM), 0.02)
    dt0 = jnp.exp(jax.random.uniform(ks[10], (DEPTH, SSD_HEADS), f32, math.log(1e-3), math.log(1e-1)))
    dt_bias = dt0 + jnp.log(-jnp.expm1(-dt0))
    a_log = jnp.log(jax.random.uniform(ks[11], (DEPTH, SSD_HEADS), f32, 1.0, 16.0))
    d_skip = 1.0 + nrm(ks[12], (DEPTH, SSD_HEADS), 0.02)
    ssd_norm_g = 1.0 + nrm(ks[13], (DEPTH, SSD_WIDTH), 0.02)
    w_out = nrm(ks[14], (DEPTH, MIX_WIDTH, D_MODEL), MIX_WIDTH ** -0.5)
    pre_ffn_g = 1.0 + nrm(ks[15], (DEPTH, D_MODEL), 0.02)
    post_ffn_g = 1.0 + nrm(ks[16], (DEPTH, D_MODEL), 0.02)
    peer_w_q = nrm(ks[17], (DEPTH, D_MODEL, PEER_HEADS * PEER_KEY_DIM), D_MODEL ** -0.5)
    peer_sub_keys = nrm(ks[18], (DEPTH, PEER_HEADS, 2, PEER_NKEYS, PEER_HALF), PEER_HALF ** -0.5)
    peer_u = nrm(ks[19], (DEPTH, PEER_NEXPERTS, D_MODEL), D_MODEL ** -0.5)
    peer_v = nrm(ks[20], (DEPTH, PEER_NEXPERTS, D_MODEL), D_MODEL ** -0.5)
    return {'x': x, 'c': c, 'positions': positions, 'w_ada': w_ada, 'b_ada': b_ada,
            'pre_mix_g': pre_mix_g, 'post_mix_g': post_mix_g, 'w_in': w_in, 'conv_w': conv_w,
            'conv_b': conv_b, 'dt_bias': dt_bias, 'a_log': a_log, 'd_skip': d_skip,
            'ssd_norm_g': ssd_norm_g, 'w_out': w_out, 'pre_ffn_g': pre_ffn_g, 'post_ffn_g': post_ffn_g,
            'peer_w_q': peer_w_q, 'peer_sub_keys': peer_sub_keys, 'peer_u': peer_u, 'peer_v': peer_v}


def reference(x, c, positions, w_ada, b_ada, pre_mix_g, post_mix_g, w_in, conv_w, conv_b, dt_bias,
              a_log, d_skip, ssd_norm_g, w_out, pre_ffn_g, post_ffn_g, peer_w_q, peer_sub_keys,
              peer_u, peer_v):
    b_, s_, _ = x.shape
    cos, sin = rope_tables(positions, HEAD_DIM)
    split_at = np.cumsum(IN_SIZES)[:-1].tolist()
    for l in range(DEPTH):
        mod = jax.nn.silu(c) @ w_ada[l] + b_ada[l]
        sh1, sc1, g1, sh2, sc2, g2 = jnp.split(mod[:, None, :], 6, axis=-1)
        h = rmsnorm(x, pre_mix_g[l]) * (1.0 + sc1) + sh1
        proj = h @ w_in[l]
        q, k, v, qi, ki, wi, z, xbc, dt_raw = jnp.split(proj, split_at, axis=-1)
        q = apply_rope(q.reshape(b_, s_, ATTN_HEADS, HEAD_DIM), cos, sin)
        k = apply_rope(k.reshape(b_, s_, ATTN_KV_HEADS, HEAD_DIM), cos, sin)
        v = v.reshape(b_, s_, ATTN_KV_HEADS, HEAD_DIM)
        qi = apply_rope(qi.reshape(b_, s_, IDX_HEADS, IDX_DIM), cos, sin)
        ki = apply_rope(ki[:, :, None, :], cos, sin)[:, :, 0, :]
        attn_out = dsa_attention(q, k, v, qi, ki, wi)
        ssd_out = ssd_mixer(z, xbc, dt_raw, conv_w[l], conv_b[l], dt_bias[l], a_log[l], d_skip[l], ssd_norm_g[l])
        mix = jnp.concatenate([attn_out, ssd_out.astype(attn_out.dtype)], axis=-1) @ w_out[l]
        x = x + g1 * rmsnorm(mix, post_mix_g[l])
        h = rmsnorm(x, pre_ffn_g[l]) * (1.0 + sc2) + sh2
        f = peer_ffn(h, peer_w_q[l], peer_sub_keys[l], peer_u[l], peer_v[l])
        x = x + g2 * rmsnorm(f, post_ffn_g[l])
    return x.astype(positions.dtype if False else x.dtype)
```

```python
import functools
import math

import jax
import jax.numpy as jnp
import numpy as np
from jax import lax
from jax.experimental import pallas as pl
from jax.experimental.pallas import tpu as pltpu

F32 = jnp.float32
BF16 = jnp.bfloat16
I32 = jnp.int32

ATTN_HEADS = 8
ATTN_KV_HEADS = 2
HEAD_DIM = 64
ATTN_WIDTH = ATTN_HEADS * HEAD_DIM
KV_WIDTH = ATTN_KV_HEADS * HEAD_DIM
IDX_HEADS = 8
IDX_DIM = 64
IDX_TOPK_MAX = 256
Q_BLOCK = 128
SSD_HEADS = 8
SSD_HEAD_DIM = 64
SSD_WIDTH = SSD_HEADS * SSD_HEAD_DIM
SSD_GROUPS = 2
SSD_STATE = 128
SSD_CONV = 4
SSD_CHUNK = 128
CONV_DIM = SSD_WIDTH + 2 * SSD_GROUPS * SSD_STATE
PEER_HEADS = 8
PEER_NKEYS = 128
PEER_KEY_DIM = 128
PEER_HALF = PEER_KEY_DIM // 2
PEER_TOPK = 16
ROPE_THETA = 10000.0
NORM_EPS = 1e-6

LANES = 128
SUBLANES = 8
VMEM_LIMIT = 56 * 1024 * 1024

_SEG = (("q", ATTN_WIDTH, ATTN_WIDTH), ("k", KV_WIDTH, KV_WIDTH), ("v", KV_WIDTH, KV_WIDTH),
        ("qi", IDX_HEADS * IDX_DIM, IDX_HEADS * IDX_DIM), ("ki", IDX_DIM, LANES), ("wi", IDX_HEADS, LANES),
        ("z", SSD_WIDTH, SSD_WIDTH), ("xbc", CONV_DIM, CONV_DIM), ("dt", SSD_HEADS, LANES))
_OFF = {}
_o = 0
for _n, _real, _pad in _SEG:
    _OFF[_n] = (_o, _real, _pad)
    _o += _pad
PROJ_PAD = _o

TOK_TILE = 256
PEER_TB = 128
ROW_WORDS = 4


def _cparams(sem):
    return pltpu.CompilerParams(dimension_semantics=sem, vmem_limit_bytes=VMEM_LIMIT)


def _rms(x, g):
    return x * lax.rsqrt(jnp.mean(x * x, axis=-1, keepdims=True) + NORM_EPS) * g


def _silu(x):
    return x * (1.0 / (1.0 + jnp.exp(-x)))


def _adaln_kernel(c_ref, w_ref, b_ref, o_ref):
    a = _silu(c_ref[...])
    o_ref[...] = jnp.dot(a, w_ref[...], preferred_element_type=F32,
                         precision=lax.Precision.HIGHEST) + b_ref[...]


def _adaln(c, w, b):
    bsz, d = c.shape
    n = w.shape[1]
    tn = 1024
    return pl.pallas_call(
        _adaln_kernel,
        grid=(n // tn,),
        in_specs=[pl.BlockSpec((bsz, d), lambda j: (0, 0)),
                  pl.BlockSpec((d, tn), lambda j: (0, j)),
                  pl.BlockSpec((1, tn), lambda j: (0, j))],
        out_specs=pl.BlockSpec((bsz, tn), lambda j: (0, j)),
        out_shape=jax.ShapeDtypeStruct((bsz, n), F32),
        compiler_params=_cparams(("arbitrary",)),
        name="adaln",
    )(c, w, b.reshape(1, n))


def _inproj_kernel(x_ref, mod_ref, g_ref, pos_ref, invf_ref, w_ref,
                   q_ref, k_ref, v_ref, qi_ref, ki_ref, wi_ref, z_ref, xbc_ref, dt_ref):
    x = x_ref[0]
    mod = mod_ref[0]
    h = _rms(x, g_ref[...]) * (1.0 + mod[1:2]) + mod[0:1]
    proj = jnp.dot(h.astype(BF16), w_ref[...], preferred_element_type=F32)

    tm = x.shape[0]
    ang = pos_ref[0] * invf_ref[...]
    cos = jnp.cos(ang)
    sin = jnp.sin(ang)
    lane = lax.broadcasted_iota(I32, (tm, LANES), 1)
    first = (lane & (HEAD_DIM // 2)) == 0
    sin_s = jnp.where(first, -sin, sin)

    def rope(xs):
        r = jnp.where(first, pltpu.roll(xs, LANES - HEAD_DIM // 2, 1), pltpu.roll(xs, HEAD_DIM // 2, 1))
        return xs * cos + r * sin_s

    def seg(name):
        o, _, p = _OFF[name]
        return proj[:, o:o + p]

    def rope_seg(name, scale):
        s = seg(name)
        parts = [rope(s[:, j * LANES:(j + 1) * LANES]) * scale for j in range(s.shape[1] // LANES)]
        return parts[0] if len(parts) == 1 else jnp.concatenate(parts, axis=1)

    q_ref[0] = rope_seg("q", HEAD_DIM ** -0.5).astype(BF16)
    k_ref[0] = rope_seg("k", 1.0).astype(BF16)
    v_ref[0] = seg("v").astype(BF16)
    qi_ref[0] = rope_seg("qi", IDX_DIM ** -0.5).astype(BF16)
    ki_ref[0] = rope_seg("ki", 1.0).astype(BF16)
    wi_ref[0] = seg("wi") * (IDX_HEADS ** -0.5)
    z_ref[0] = seg("z")
    xbc_ref[0] = seg("xbc")
    dt_ref[0] = seg("dt")


def _inproj(x, mod3, g, pos3, invf, w_pad):
    b, s, d = x.shape
    tm = min(TOK_TILE, s)
    tok = lambda width: pl.BlockSpec((1, tm, width), lambda bi, i: (bi, i, 0))
    outs = (("q", BF16), ("k", BF16), ("v", BF16), ("qi", BF16), ("ki", BF16),
            ("wi", F32), ("z", F32), ("xbc", F32), ("dt", F32))
    return pl.pallas_call(
        _inproj_kernel,
        grid=(b, s // tm),
        in_specs=[tok(d),
                  pl.BlockSpec((1, 6, d), lambda bi, i: (bi, 0, 0)),
                  pl.BlockSpec((1, d), lambda bi, i: (0, 0)),
                  tok(1),
                  pl.BlockSpec((1, LANES), lambda bi, i: (0, 0)),
                  pl.BlockSpec((d, PROJ_PAD), lambda bi, i: (0, 0))],
        out_specs=[tok(_OFF[n][2]) for n, _ in outs],
        out_shape=[jax.ShapeDtypeStruct((b, s, _OFF[n][2]), dt) for n, dt in outs],
        compiler_params=_cparams(("arbitrary", "arbitrary")),
        name="inproj",
    )(x, mod3, g, pos3, invf, w_pad)


_INT_MIN = -2 ** 31


def _dsa_kernel(qi_ref, ki_ref, wi_ref, q_ref, k_ref, v_ref, o_ref, *, topk):
    i = pl.program_id(1)
    s_len = ki_ref.shape[1]
    nq = Q_BLOCK
    qi = qi_ref[0]
    ki = ki_ref[0][:, :IDX_DIM]
    w = wi_ref[0]
    nt = (((1,), (1,)), ((), ()))

    score = jnp.zeros((nq, s_len), F32)
    for h in range(IDX_HEADS):
        lg = lax.dot_general(qi[:, h * IDX_DIM:(h + 1) * IDX_DIM], ki, nt, preferred_element_type=F32)
        score = score + w[:, h:h + 1] * jnp.maximum(lg, 0.0)

    qpos = i * nq + lax.broadcasted_iota(I32, (nq, s_len), 0)
    kpos = lax.broadcasted_iota(I32, (nq, s_len), 1)
    causal = kpos <= qpos
    score = jnp.where(causal, score, -jnp.inf)

    bits = lax.bitcast_convert_type(score, I32)
    key = jnp.where(bits < 0, bits ^ jnp.int32(0x7FFFFFFF), bits)
    key = jnp.where(bits == jnp.int32(_INT_MIN), 0, key)

    kf = float(topk)

    def count(mask):
        return jnp.sum(jnp.where(mask, 1.0, 0.0), axis=1, keepdims=True)

    def bs_body(it, ans):
        cand = ans | jnp.left_shift(jnp.int32(1), 31 - it)
        thr = cand ^ jnp.int32(_INT_MIN)
        return jnp.where(count(key >= thr) >= kf, cand, ans)

    ans = lax.fori_loop(0, 32, bs_body, jnp.zeros((nq, 1), I32))
    thr = ans ^ jnp.int32(_INT_MIN)

    gt = key > thr
    eq = key == thr
    need = kf - count(gt)
    eqb = jnp.where(eq, 1.0, 0.0).astype(BF16)
    r_i = lax.broadcasted_iota(I32, (LANES, LANES), 0)
    c_i = lax.broadcasted_iota(I32, (LANES, LANES), 1)
    tri = jnp.where(r_i <= c_i, 1.0, 0.0).astype(BF16)
    ones = jnp.ones((LANES, LANES), BF16)
    carry = jnp.zeros((nq, LANES), F32)
    prefix = []
    for c in range(s_len // LANES):
        ec = eqb[:, c * LANES:(c + 1) * LANES]
        prefix.append(jnp.dot(ec, tri, preferred_element_type=F32) + carry)
        carry = carry + jnp.dot(ec, ones, preferred_element_type=F32)
    keep = jnp.concatenate(prefix, axis=1) <= need
    sel = jnp.logical_and(causal, jnp.logical_or(gt, jnp.logical_and(eq, keep)))
    bias = jnp.where(sel, 0.0, -jnp.inf)

    q = q_ref[0]
    k = k_ref[0]
    v = v_ref[0]
    grp = ATTN_HEADS // ATTN_KV_HEADS
    outs = []
    for h in range(ATTN_HEADS):
        g = h // grp
        kg = k[:, g * HEAD_DIM:(g + 1) * HEAD_DIM]
        vg = v[:, g * HEAD_DIM:(g + 1) * HEAD_DIM]
        sc = lax.dot_general(q[:, h * HEAD_DIM:(h + 1) * HEAD_DIM], kg, nt, preferred_element_type=F32) + bias
        m = jnp.max(sc, axis=1, keepdims=True)
        p = jnp.exp(sc - m)
        l = jnp.sum(p, axis=1, keepdims=True)
        o = jnp.dot(p.astype(BF16), vg, preferred_element_type=F32)
        outs.append(o * (1.0 / l))
    o_ref[0] = jnp.concatenate(outs, axis=1).astype(BF16)


def _dsa(qi, ki, wi, q, k, v):
    b, s, _ = q.shape
    topk = min(IDX_TOPK_MAX, s // 4)
    blk = lambda width: pl.BlockSpec((1, Q_BLOCK, width), lambda bi, i: (bi, i, 0))
    full = lambda width: pl.BlockSpec((1, s, width), lambda bi, i: (bi, 0, 0))
    return pl.pallas_call(
        functools.partial(_dsa_kernel, topk=topk),
        grid=(b, s // Q_BLOCK),
        in_specs=[blk(IDX_HEADS * IDX_DIM), full(LANES), blk(LANES), blk(ATTN_WIDTH), full(KV_WIDTH), full(KV_WIDTH)],
        out_specs=blk(ATTN_WIDTH),
        out_shape=jax.ShapeDtypeStruct((b, s, ATTN_WIDTH), BF16),
        compiler_params=_cparams(("arbitrary", "arbitrary")),
        name="dsa",
    )(qi, ki, wi, q, k, v)


def _ssd_kernel(z_ref, xbc_ref, dt_ref, cw_ref, cb_ref, dtb_ref, alog_ref, dsk_ref, ng_ref,
                o_ref, xs_ref, st_ref):
    c = pl.program_id(1)
    L = SSD_CHUNK
    halo = SUBLANES

    @pl.when(c == 0)
    def _():
        xs_ref[0:halo, :] = jnp.zeros((halo, CONV_DIM), F32)
        st_ref[...] = jnp.zeros(st_ref.shape, F32)

    xs_ref[halo:halo + L, :] = xbc_ref[0]
    cw = cw_ref[...]
    acc = cb_ref[...]
    for j in range(SSD_CONV):
        o = halo - (SSD_CONV - 1) + j
        acc = acc + xs_ref[o:o + L, :] * cw[j:j + 1, :]
    xs_ref[0:halo, :] = xs_ref[L:L + halo, :]
    xa = _silu(acc)

    xh = xa[:, :SSD_WIDTH]
    bm = xa[:, SSD_WIDTH:SSD_WIDTH + SSD_GROUPS * SSD_STATE]
    cm = xa[:, SSD_WIDTH + SSD_GROUPS * SSD_STATE:]

    dtr = dt_ref[0] + dtb_ref[...]
    dt = jnp.maximum(dtr, 0.0) + jnp.log(1.0 + jnp.exp(-jnp.abs(dtr)))
    a = -jnp.exp(alog_ref[...])
    adt = dt * a
    r_i = lax.broadcasted_iota(I32, (L, L), 0)
    c_i = lax.broadcasted_iota(I32, (L, L), 1)
    lower = r_i >= c_i
    tril = jnp.where(lower, 1.0, 0.0)
    acum = jnp.dot(tril, adt, preferred_element_type=F32, precision=lax.Precision.HIGHEST)
    acum_t = acum.T
    dsk = dsk_ref[...]

    nt = (((1,), (1,)), ((), ()))
    rr = SSD_HEADS // SSD_GROUPS
    ys = []
    for g in range(SSD_GROUPS):
        cg = cm[:, g * SSD_STATE:(g + 1) * SSD_STATE]
        bg = bm[:, g * SSD_STATE:(g + 1) * SSD_STATE]
        cgb = cg.astype(BF16)
        bgb = bg.astype(BF16)
        cb = lax.dot_general(cgb, bgb, nt, preferred_element_type=F32)
        bgt = bg.T.astype(BF16)
        for r in range(rr):
            h = g * rr + r
            ac = acum[:, h:h + 1]
            ar = acum_t[h:h + 1, :]
            lmat = jnp.where(lower, jnp.exp(ac - ar), 0.0)
            xhh = xh[:, h * SSD_HEAD_DIM:(h + 1) * SSD_HEAD_DIM]
            xdt = xhh * dt[:, h:h + 1]
            yd = jnp.dot((cb * lmat).astype(BF16), xdt.astype(BF16), preferred_element_type=F32)
            st = st_ref[h]
            yo = jnp.dot(cgb, st.astype(BF16), preferred_element_type=F32) * jnp.exp(ac)
            ys.append(yd + yo + xhh * jnp.broadcast_to(dsk[0:1, h:h + 1], (L, 1)))
            alast = jnp.broadcast_to(acum[L - 1:L, h:h + 1], (L, 1))
            decay = jnp.exp(alast - ac)
            st_ref[h] = jnp.exp(alast) * st + jnp.dot(bgt, (xdt * decay).astype(BF16), preferred_element_type=F32)
    y = jnp.concatenate(ys, axis=1)
    y = y * _silu(z_ref[0])
    o_ref[0] = _rms(y, ng_ref[...]).astype(BF16)


def _ssd(z, xbc, dt, conv_w, conv_b, dt_bias_p, a_log_p, d_skip_p, norm_g):
    b, s, _ = z.shape
    blk = lambda width: pl.BlockSpec((1, SSD_CHUNK, width), lambda bi, i: (bi, i, 0))
    const = lambda r, width: pl.BlockSpec((r, width), lambda bi, i: (0, 0))
    return pl.pallas_call(
        _ssd_kernel,
        grid=(b, s // SSD_CHUNK),
        in_specs=[blk(SSD_WIDTH), blk(CONV_DIM), blk(LANES), const(SSD_CONV, CONV_DIM), const(1, CONV_DIM),
                  const(1, LANES), const(1, LANES), const(1, LANES), const(1, SSD_WIDTH)],
        out_specs=blk(SSD_WIDTH),
        out_shape=jax.ShapeDtypeStruct((b, s, SSD_WIDTH), BF16),
        scratch_shapes=[pltpu.VMEM((SSD_CHUNK + 2 * SUBLANES, CONV_DIM), F32),
                        pltpu.VMEM((SSD_HEADS, SSD_STATE, SSD_HEAD_DIM), F32)],
        compiler_params=_cparams(("arbitrary", "arbitrary")),
        name="ssd",
    )(z, xbc, dt, conv_w, conv_b, dt_bias_p, a_log_p, d_skip_p, norm_g)


def _outproj_kernel(attn_ref, ssd_ref, x_ref, mod_ref, pmg_ref, pfg_ref, wo_ref, wq_ref,
                    x1_ref, xlo_ref, xhi_ref, qp_ref):
    mod = mod_ref[0]
    wo = wo_ref[...]
    mix = (jnp.dot(attn_ref[0], wo[:ATTN_WIDTH], preferred_element_type=F32)
           + jnp.dot(ssd_ref[0], wo[ATTN_WIDTH:], preferred_element_type=F32))
    x1 = x_ref[0] + mod[2:3] * _rms(mix, pmg_ref[...])
    x1_ref[0] = x1
    h2 = _rms(x1, pfg_ref[...]) * (1.0 + mod[4:5]) + mod[3:4]
    half = h2.shape[1] // 2
    xlo_ref[0] = h2[:, :half]
    xhi_ref[0] = h2[:, half:]
    qp_ref[0] = jnp.dot(h2.astype(BF16), wq_ref[...], preferred_element_type=F32).astype(BF16)


def _outproj(attn, ssd, x, mod3, pmg, pfg, wo, wq):
    b, s, d = x.shape
    tm = min(TOK_TILE, s)
    tok = lambda width: pl.BlockSpec((1, tm, width), lambda bi, i: (bi, i, 0))
    const = lambda r, width: pl.BlockSpec((r, width), lambda bi, i: (0, 0))
    return pl.pallas_call(
        _outproj_kernel,
        grid=(b, s // tm),
        in_specs=[tok(ATTN_WIDTH), tok(SSD_WIDTH), tok(d),
                  pl.BlockSpec((1, 6, d), lambda bi, i: (bi, 0, 0)),
                  const(1, d), const(1, d), const(wo.shape[0], d), const(d, wq.shape[1])],
        out_specs=[tok(d), tok(d // 2), tok(d // 2), tok(wq.shape[1])],
        out_shape=[jax.ShapeDtypeStruct((b, s, d), F32),
                   jax.ShapeDtypeStruct((b, s, d // 2), F32),
                   jax.ShapeDtypeStruct((b, s, d // 2), F32),
                   jax.ShapeDtypeStruct((b, s, wq.shape[1]), BF16)],
        compiler_params=_cparams(("arbitrary", "arbitrary")),
        name="outproj",
    )(attn, ssd, x, mod3, pmg, pfg, wo, wq)


def _extract_top(vals, idx_payload, n_out, big):
    rows = vals.shape[0]
    rid = lax.broadcasted_iota(I32, vals.shape, 0)
    out_v = []
    out_p = [[] for _ in idx_payload]
    for _ in range(n_out):
        m = jnp.max(vals, axis=0, keepdims=True)
        first = jnp.min(jnp.where(vals == m, rid, big), axis=0, keepdims=True)
        hit = rid == first
        out_v.append(m)
        for j, p in enumerate(idx_payload):
            out_p[j].append(jnp.max(jnp.where(hit, p, -1), axis=0, keepdims=True))
        vals = jnp.where(hit, -jnp.inf, vals)
    return jnp.concatenate(out_v, axis=0), [jnp.concatenate(p, axis=0) for p in out_p]


def _route_kernel(qp_ref, keys_ref, eidx_ref, gate_ref):
    qp = qp_ref[...]
    nt = (((1,), (1,)), ((), ()))
    kid = lax.broadcasted_iota(I32, (PEER_NKEYS, qp.shape[0]), 0)
    e_rows = []
    g_rows = []
    for h in range(PEER_HEADS):
        sv = []
        si = []
        for i in range(2):
            o = h * PEER_KEY_DIM + i * PEER_HALF
            st = lax.dot_general(keys_ref[h, i], qp[:, o:o + PEER_HALF], nt, preferred_element_type=F32)
            v, (ix,) = _extract_top(st, [kid], PEER_TOPK, PEER_NKEYS)
            sv.append(v)
            si.append(ix)
        cand = []
        cidx = []
        for a in range(PEER_TOPK):
            cand.append(sv[0][a:a + 1, :] + sv[1])
            cidx.append(si[0][a:a + 1, :] * PEER_NKEYS + si[1])
        cand = jnp.concatenate(cand, axis=0)
        cidx = jnp.concatenate(cidx, axis=0)
        fv, (eid,) = _extract_top(cand, [cidx], PEER_TOPK, PEER_TOPK * PEER_TOPK)
        p = jnp.exp(fv - fv[0:1, :])
        g_rows.append(p * (1.0 / jnp.sum(p, axis=0, keepdims=True)))
        e_rows.append(eid)
    eidx_ref[0] = jnp.concatenate(e_rows, axis=0)
    gate_ref[0] = jnp.concatenate(g_rows, axis=0)


def _route(qp2, keys_b):
    t = qp2.shape[0]
    nb = t // PEER_TB
    slots = PEER_HEADS * PEER_TOPK
    return pl.pallas_call(
        _route_kernel,
        grid=(nb,),
        in_specs=[pl.BlockSpec((PEER_TB, qp2.shape[1]), lambda i: (i, 0)),
                  pl.BlockSpec(keys_b.shape, lambda i: (0, 0, 0, 0))],
        out_specs=[pl.BlockSpec((1, slots, PEER_TB), lambda i: (i, 0, 0)),
                   pl.BlockSpec((1, slots, PEER_TB), lambda i: (i, 0, 0))],
        out_shape=[jax.ShapeDtypeStruct((nb, slots, PEER_TB), I32),
                   jax.ShapeDtypeStruct((nb, slots, PEER_TB), F32)],
        compiler_params=_cparams(("arbitrary",)),
        name="peer_route",
    )(qp2, keys_b)


def _unpack_pair(w):
    lo = lax.bitcast_convert_type(jnp.left_shift(w, 16), F32)
    hi = lax.bitcast_convert_type(w & jnp.int32(-65536), F32)
    return lo, hi


def _gather_row(tab_ref, e):
    return tab_ref[pl.ds(pl.multiple_of(e * ROW_WORDS, ROW_WORDS), ROW_WORDS), :]


def _gelu_exact(x):
    return 0.5 * x * (1.0 + lax.erf(x * (2.0 ** -0.5)))


def _peer_u_kernel(idx_ref, xlo_ref, xhi_ref, gate_ref, tab_ref, act_ref):
    slots = gate_ref.shape[1]
    tb = gate_ref.shape[2]
    lane = lax.broadcasted_iota(I32, (slots, tb), 1)
    sub = lax.broadcasted_iota(I32, (SUBLANES, LANES), 0)
    m2 = (sub & 2) == 0
    m1 = (sub & 1) == 0

    def token(t, acc):
        r0 = pl.multiple_of(t * ROW_WORDS, ROW_WORDS)
        xlo = xlo_ref[pl.ds(r0, ROW_WORDS), :]
        xhi = xhi_ref[pl.ds(r0, ROW_WORDS), :]
        zs = []
        for g in range(slots // SUBLANES):
            prods = []
            for k in range(SUBLANES):
                lo, hi = _unpack_pair(_gather_row(tab_ref, idx_ref[0, g * SUBLANES + k, t]))
                prods.append(lo * xlo + hi * xhi)
            v = [jnp.concatenate([prods[j], prods[j + 4]], axis=0) for j in range(4)]
            w = [x + pltpu.roll(x, 6, 0) for x in v]
            xx = [jnp.where(m2, w[j], pltpu.roll(w[j + 2], 2, 0)) for j in range(2)]
            y = [x + pltpu.roll(x, 7, 0) for x in xx]
            zs.append(jnp.where(m1, y[0], pltpu.roll(y[1], 1, 0)))
        col = jnp.sum(jnp.concatenate(zs, axis=0), axis=1, keepdims=True)
        return jnp.where(lane == t, col, acc)

    s = lax.fori_loop(0, tb, token, jnp.zeros((slots, tb), F32))
    act_ref[0] = _gelu_exact(s) * gate_ref[0]


def _peer_v_kernel(idx_ref, act_ref, tab_ref, flo_ref, fhi_ref):
    slots = idx_ref.shape[1]
    tb = idx_ref.shape[2]

    def token(t, carry):
        lo_acc = jnp.zeros((ROW_WORDS, LANES), F32)
        hi_acc = jnp.zeros((ROW_WORDS, LANES), F32)
        for j in range(slots):
            lo, hi = _unpack_pair(_gather_row(tab_ref, idx_ref[0, j, t]))
            a = act_ref[0, j, t]
            lo_acc = lo_acc + a * lo
            hi_acc = hi_acc + a * hi
        r0 = pl.multiple_of(t * ROW_WORDS, ROW_WORDS)
        flo_ref[pl.ds(r0, ROW_WORDS), :] = lo_acc
        fhi_ref[pl.ds(r0, ROW_WORDS), :] = hi_acc
        return carry

    lax.fori_loop(0, tb, token, 0)


def _table_spec(tab):
    return pl.BlockSpec(tab.shape, lambda i: (0, 0), pipeline_mode=pl.Buffered(1))


def _peer_u(eidx, xlo4, xhi4, gate, tab):
    nb, slots, tb = eidx.shape
    smem = pl.BlockSpec((1, slots, tb), lambda i: (i, 0, 0), memory_space=pltpu.SMEM)
    rows = pl.BlockSpec((tb * ROW_WORDS, LANES), lambda i: (i, 0))
    tile = pl.BlockSpec((1, slots, tb), lambda i: (i, 0, 0))
    return pl.pallas_call(
        _peer_u_kernel,
        grid=(nb,),
        in_specs=[smem, rows, rows, tile, _table_spec(tab)],
        out_specs=tile,
        out_shape=jax.ShapeDtypeStruct((nb, slots, tb), F32),
        compiler_params=_cparams(("arbitrary",)),
        name="peer_u",
    )(eidx, xlo4, xhi4, gate, tab)


def _peer_v(eidx, act, tab):
    nb, slots, tb = eidx.shape
    smem = pl.BlockSpec((1, slots, tb), lambda i: (i, 0, 0), memory_space=pltpu.SMEM)
    rows = pl.BlockSpec((tb * ROW_WORDS, LANES), lambda i: (i, 0))
    return pl.pallas_call(
        _peer_v_kernel,
        grid=(nb,),
        in_specs=[smem, smem, _table_spec(tab)],
        out_specs=[rows, rows],
        out_shape=[jax.ShapeDtypeStruct((nb * tb * ROW_WORDS, LANES), F32)] * 2,
        compiler_params=_cparams(("arbitrary",)),
        name="peer_v",
    )(eidx, act, tab)


def _pack_table(w):
    e, d = w.shape
    wb = lax.bitcast_convert_type(w.astype(BF16), jnp.uint16).astype(jnp.uint32)
    lo = wb[:, :d // 2]
    hi = wb[:, d // 2:]
    packed = lax.bitcast_convert_type(lo | (hi << 16), I32)
    return packed.reshape(e * ROW_WORDS, LANES)


def _final_kernel(x1_ref, flo_ref, fhi_ref, mod_ref, g_ref, o_ref):
    f = jnp.concatenate([flo_ref[0], fhi_ref[0]], axis=1)
    o_ref[0] = x1_ref[0] + mod_ref[0][5:6] * _rms(f, g_ref[...])


def _final(x1, flo, fhi, mod3, g):
    b, s, d = x1.shape
    tm = min(TOK_TILE, s)
    tok = lambda width: pl.BlockSpec((1, tm, width), lambda bi, i: (bi, i, 0))
    return pl.pallas_call(
        _final_kernel,
        grid=(b, s // tm),
        in_specs=[tok(d), tok(d // 2), tok(d // 2),
                  pl.BlockSpec((1, 6, d), lambda bi, i: (bi, 0, 0)),
                  pl.BlockSpec((1, d), lambda bi, i: (0, 0))],
        out_specs=tok(d),
        out_shape=jax.ShapeDtypeStruct((b, s, d), F32),
        compiler_params=_cparams(("arbitrary", "arbitrary")),
        name="final",
    )(x1, flo, fhi, mod3, g)


def _pad_in_weights(w_in):
    cols = []
    src = 0
    for _, real, pad in _SEG:
        blk = w_in[:, src:src + real]
        if pad > real:
            blk = jnp.pad(blk, ((0, 0), (0, pad - real)))
        cols.append(blk)
        src += real
    return jnp.concatenate(cols, axis=1).astype(BF16)


def _pad_lanes(v):
    return jnp.pad(v.reshape(1, -1), ((0, 0), (0, LANES - v.shape[-1])))


def kernel(x, c, positions, w_ada, b_ada, pre_mix_g, post_mix_g, w_in, conv_w, conv_b, dt_bias, a_log,
           d_skip, ssd_norm_g, w_out, pre_ffn_g, post_ffn_g, peer_w_q, peer_sub_keys, peer_u, peer_v):
    b, s, d = x.shape
    depth = w_ada.shape[0]
    inv_freq = ROPE_THETA ** (-jnp.arange(0, HEAD_DIM, 2, dtype=F32) / HEAD_DIM)
    invf = jnp.tile(inv_freq, LANES // inv_freq.shape[0]).reshape(1, LANES)
    pos3 = positions.astype(F32)[..., None]
    for l in range(depth):
        mod3 = _adaln(c, w_ada[l], b_ada[l]).reshape(b, 6, d)
        q, k, v, qi, ki, wi, z, xbc, dt = _inproj(
            x, mod3, pre_mix_g[l].reshape(1, d), pos3, invf, _pad_in_weights(w_in[l]))
        attn = _dsa(qi, ki, wi, q, k, v)
        ssd = _ssd(z, xbc, dt, conv_w[l], conv_b[l].reshape(1, -1), _pad_lanes(dt_bias[l]), _pad_lanes(a_log[l]),
                   _pad_lanes(d_skip[l]), ssd_norm_g[l].reshape(1, -1))
        x1, xlo, xhi, qp = _outproj(attn, ssd, x, mod3, post_mix_g[l].reshape(1, d), pre_ffn_g[l].reshape(1, d),
                                    w_out[l].astype(BF16), peer_w_q[l].astype(BF16))
        t = b * s
        eidx, gate = _route(qp.reshape(t, -1), peer_sub_keys[l].astype(BF16))
        act = _peer_u(eidx, xlo.reshape(t * ROW_WORDS, LANES), xhi.reshape(t * ROW_WORDS, LANES), gate,
                      _pack_table(peer_u[l]))
        flo, fhi = _peer_v(eidx, act, _pack_table(peer_v[l]))
        x = _final(x1, flo.reshape(b, s, d // 2), fhi.reshape(b, s, d // 2), mod3, post_ffn_g[l].reshape(1, d))
    return x
```

```python
import functools
import math

import jax
import jax.numpy as jnp
import numpy as np
from jax import lax
from jax.experimental import pallas as pl
from jax.experimental.pallas import tpu as pltpu

F32 = jnp.float32
BF16 = jnp.bfloat16
I32 = jnp.int32

ATTN_HEADS = 8
ATTN_KV_HEADS = 2
HEAD_DIM = 64
ATTN_WIDTH = ATTN_HEADS * HEAD_DIM
KV_WIDTH = ATTN_KV_HEADS * HEAD_DIM
IDX_HEADS = 8
IDX_DIM = 64
IDX_TOPK_MAX = 256
Q_BLOCK = 128
SSD_HEADS = 8
SSD_HEAD_DIM = 64
SSD_WIDTH = SSD_HEADS * SSD_HEAD_DIM
SSD_GROUPS = 2
SSD_STATE = 128
SSD_CONV = 4
SSD_CHUNK = 128
CONV_DIM = SSD_WIDTH + 2 * SSD_GROUPS * SSD_STATE
PEER_HEADS = 8
PEER_NKEYS = 128
PEER_KEY_DIM = 128
PEER_HALF = PEER_KEY_DIM // 2
PEER_TOPK = 16
ROPE_THETA = 10000.0
NORM_EPS = 1e-6

LANES = 128
SUBLANES = 8
VMEM_LIMIT = 56 * 1024 * 1024

_SEG = (("q", ATTN_WIDTH, ATTN_WIDTH), ("k", KV_WIDTH, KV_WIDTH), ("v", KV_WIDTH, KV_WIDTH),
        ("qi", IDX_HEADS * IDX_DIM, IDX_HEADS * IDX_DIM), ("ki", IDX_DIM, LANES), ("wi", IDX_HEADS, LANES),
        ("z", SSD_WIDTH, SSD_WIDTH), ("xbc", CONV_DIM, CONV_DIM), ("dt", SSD_HEADS, LANES))
_OFF = {}
_o = 0
for _n, _real, _pad in _SEG:
    _OFF[_n] = (_o, _real, _pad)
    _o += _pad
PROJ_PAD = _o

TOK_TILE = 256
PEER_TB = 128
ROW_WORDS = 4


def _cparams(sem):
    return pltpu.CompilerParams(dimension_semantics=sem, vmem_limit_bytes=VMEM_LIMIT)


def _rms(x, g):
    return x * lax.rsqrt(jnp.mean(x * x, axis=-1, keepdims=True) + NORM_EPS) * g


def _silu(x):
    return x * (1.0 / (1.0 + jnp.exp(-x)))


def _adaln_kernel(c_ref, w_ref, b_ref, o_ref):
    a = _silu(c_ref[...])
    o_ref[...] = jnp.dot(a, w_ref[...], preferred_element_type=F32,
                         precision=lax.Precision.HIGHEST) + b_ref[...]


def _adaln(c, w, b):
    bsz, d = c.shape
    n = w.shape[1]
    tn = 1024
    return pl.pallas_call(
        _adaln_kernel,
        grid=(n // tn,),
        in_specs=[pl.BlockSpec((bsz, d), lambda j: (0, 0)),
                  pl.BlockSpec((d, tn), lambda j: (0, j)),
                  pl.BlockSpec((1, tn), lambda j: (0, j))],
        out_specs=pl.BlockSpec((bsz, tn), lambda j: (0, j)),
        out_shape=jax.ShapeDtypeStruct((bsz, n), F32),
        compiler_params=_cparams(("arbitrary",)),
        name="adaln",
    )(c, w, b.reshape(1, n))


def _inproj_kernel(x_ref, mod_ref, g_ref, pos_ref, invf_ref, w_ref,
                   q_ref, k_ref, v_ref, qi_ref, ki_ref, wi_ref, z_ref, xbc_ref, dt_ref):
    x = x_ref[0]
    mod = mod_ref[0]
    h = _rms(x, g_ref[...]) * (1.0 + mod[1:2]) + mod[0:1]
    proj = jnp.dot(h.astype(BF16), w_ref[...], preferred_element_type=F32)

    tm = x.shape[0]
    ang = pos_ref[0] * invf_ref[...]
    cos = jnp.cos(ang)
    sin = jnp.sin(ang)
    lane = lax.broadcasted_iota(I32, (tm, LANES), 1)
    first = (lane & (HEAD_DIM // 2)) == 0
    sin_s = jnp.where(first, -sin, sin)

    def rope(xs):
        r = jnp.where(first, pltpu.roll(xs, LANES - HEAD_DIM // 2, 1), pltpu.roll(xs, HEAD_DIM // 2, 1))
        return xs * cos + r * sin_s

    def seg(name):
        o, _, p = _OFF[name]
        return proj[:, o:o + p]

    def rope_seg(name, scale):
        s = seg(name)
        parts = [rope(s[:, j * LANES:(j + 1) * LANES]) * scale for j in range(s.shape[1] // LANES)]
        return parts[0] if len(parts) == 1 else jnp.concatenate(parts, axis=1)

    q_ref[0] = rope_seg("q", HEAD_DIM ** -0.5).astype(BF16)
    k_ref[0] = rope_seg("k", 1.0).astype(BF16)
    v_ref[0] = seg("v").astype(BF16)
    qi_ref[0] = rope_seg("qi", IDX_DIM ** -0.5).astype(BF16)
    ki_ref[0] = rope_seg("ki", 1.0).astype(BF16)
    wi_ref[0] = seg("wi") * (IDX_HEADS ** -0.5)
    z_ref[0] = seg("z")
    xbc_ref[0] = seg("xbc")
    dt_ref[0] = seg("dt")


def _inproj(x, mod3, g, pos3, invf, w_pad):
    b, s, d = x.shape
    tm = min(TOK_TILE, s)
    tok = lambda width: pl.BlockSpec((1, tm, width), lambda bi, i: (bi, i, 0))
    outs = (("q", BF16), ("k", BF16), ("v", BF16), ("qi", BF16), ("ki", BF16),
            ("wi", F32), ("z", F32), ("xbc", F32), ("dt", F32))
    return pl.pallas_call(
        _inproj_kernel,
        grid=(b, s // tm),
        in_specs=[tok(d),
                  pl.BlockSpec((1, 6, d), lambda bi, i: (bi, 0, 0)),
                  pl.BlockSpec((1, d), lambda bi, i: (0, 0)),
                  tok(1),
                  pl.BlockSpec((1, LANES), lambda bi, i: (0, 0)),
                  pl.BlockSpec((d, PROJ_PAD), lambda bi, i: (0, 0))],
        out_specs=[tok(_OFF[n][2]) for n, _ in outs],
        out_shape=[jax.ShapeDtypeStruct((b, s, _OFF[n][2]), dt) for n, dt in outs],
        compiler_params=_cparams(("arbitrary", "arbitrary")),
        name="inproj",
    )(x, mod3, g, pos3, invf, w_pad)


_INT_MIN = -2 ** 31


DSA_KEY_SPAN = 512


def _dsa_kernel(qi_ref, ki_ref, wi_ref, q_ref, k_ref, v_ref, o_ref, *, topk):
    i = pl.program_id(1)
    span = min(DSA_KEY_SPAN, ki_ref.shape[1])
    n_span = ki_ref.shape[1] // span
    need_spans = (i * Q_BLOCK) // span + 1
    for c in range(1, n_span + 1):
        @pl.when(need_spans == c)
        def _():
            _dsa_block(qi_ref, ki_ref, wi_ref, q_ref, k_ref, v_ref, o_ref, topk=topk, s_len=c * span)


def _dsa_block(qi_ref, ki_ref, wi_ref, q_ref, k_ref, v_ref, o_ref, *, topk, s_len):
    i = pl.program_id(1)
    nq = Q_BLOCK
    qi = qi_ref[0]
    ki = ki_ref[0, 0:s_len, 0:IDX_DIM]
    w = wi_ref[0]
    nt = (((1,), (1,)), ((), ()))

    score = jnp.zeros((nq, s_len), F32)
    for h in range(IDX_HEADS):
        lg = lax.dot_general(qi[:, h * IDX_DIM:(h + 1) * IDX_DIM], ki, nt, preferred_element_type=F32)
        score = score + w[:, h:h + 1] * jnp.maximum(lg, 0.0)

    qpos = i * nq + lax.broadcasted_iota(I32, (nq, s_len), 0)
    kpos = lax.broadcasted_iota(I32, (nq, s_len), 1)
    causal = kpos <= qpos
    score = jnp.where(causal, score, -jnp.inf)

    kf = float(topk)

    def count(mask):
        return jnp.sum(jnp.where(mask, 1.0, 0.0), axis=1, keepdims=True)

    def threshold(u):
        key = u ^ jnp.int32(_INT_MIN)
        return lax.bitcast_convert_type(jnp.where(key < 0, key ^ jnp.int32(0x7FFFFFFF), key), F32)

    def bs_body(it, ans):
        cand = ans | jnp.left_shift(jnp.int32(1), 31 - it)
        return jnp.where(count(score >= threshold(cand)) >= kf, cand, ans)

    n_bits = jnp.where((i + 1) * nq <= topk, 0, 32)
    ans = lax.fori_loop(0, n_bits, bs_body, jnp.zeros((nq, 1), I32))
    thr = threshold(ans)

    gt = score > thr
    eq = score == thr
    take_all = count(score >= thr) < kf
    need = kf - count(gt)
    eqb = jnp.where(eq, 1.0, 0.0).astype(BF16)
    r_i = lax.broadcasted_iota(I32, (LANES, LANES), 0)
    c_i = lax.broadcasted_iota(I32, (LANES, LANES), 1)
    tri = jnp.where(r_i <= c_i, 1.0, 0.0).astype(BF16)
    ones = jnp.ones((LANES, LANES), BF16)
    carry = jnp.zeros((nq, LANES), F32)
    prefix = []
    for c in range(s_len // LANES):
        ec = eqb[:, c * LANES:(c + 1) * LANES]
        prefix.append(jnp.dot(ec, tri, preferred_element_type=F32) + carry)
        carry = carry + jnp.dot(ec, ones, preferred_element_type=F32)
    keep = jnp.concatenate(prefix, axis=1) <= need
    sel = jnp.logical_and(causal, jnp.logical_or(jnp.logical_or(take_all, gt), jnp.logical_and(eq, keep)))
    bias = jnp.where(sel, 0.0, -jnp.inf)

    q = q_ref[0]
    k = k_ref[0, 0:s_len, :]
    v = v_ref[0, 0:s_len, :]
    grp = ATTN_HEADS // ATTN_KV_HEADS
    outs = []
    for h in range(ATTN_HEADS):
        g = h // grp
        kg = k[:, g * HEAD_DIM:(g + 1) * HEAD_DIM]
        vg = v[:, g * HEAD_DIM:(g + 1) * HEAD_DIM]
        sc = lax.dot_general(q[:, h * HEAD_DIM:(h + 1) * HEAD_DIM], kg, nt, preferred_element_type=F32) + bias
        m = jnp.max(sc, axis=1, keepdims=True)
        p = jnp.exp(sc - m)
        l = jnp.sum(p, axis=1, keepdims=True)
        o = jnp.dot(p.astype(BF16), vg, preferred_element_type=F32)
        outs.append(o * (1.0 / l))
    o_ref[0] = jnp.concatenate(outs, axis=1).astype(BF16)


def _dsa(qi, ki, wi, q, k, v):
    b, s, _ = q.shape
    topk = min(IDX_TOPK_MAX, s // 4)
    blk = lambda width: pl.BlockSpec((1, Q_BLOCK, width), lambda bi, i: (bi, i, 0))
    full = lambda width: pl.BlockSpec((1, s, width), lambda bi, i: (bi, 0, 0))
    return pl.pallas_call(
        functools.partial(_dsa_kernel, topk=topk),
        grid=(b, s // Q_BLOCK),
        in_specs=[blk(IDX_HEADS * IDX_DIM), full(LANES), blk(LANES), blk(ATTN_WIDTH), full(KV_WIDTH), full(KV_WIDTH)],
        out_specs=blk(ATTN_WIDTH),
        out_shape=jax.ShapeDtypeStruct((b, s, ATTN_WIDTH), BF16),
        compiler_params=_cparams(("arbitrary", "arbitrary")),
        name="dsa",
    )(qi, ki, wi, q, k, v)


def _ssd_kernel(z_ref, xbc_ref, dt_ref, cw_ref, cb_ref, dtb_ref, alog_ref, dsk_ref, ng_ref,
                o_ref, xs_ref, st_ref):
    c = pl.program_id(1)
    L = SSD_CHUNK
    halo = SUBLANES

    @pl.when(c == 0)
    def _():
        xs_ref[0:halo, :] = jnp.zeros((halo, CONV_DIM), F32)
        st_ref[...] = jnp.zeros(st_ref.shape, F32)

    xs_ref[halo:halo + L, :] = xbc_ref[0]
    cw = cw_ref[...]
    acc = cb_ref[...]
    for j in range(SSD_CONV):
        o = halo - (SSD_CONV - 1) + j
        acc = acc + xs_ref[o:o + L, :] * cw[j:j + 1, :]
    xs_ref[0:halo, :] = xs_ref[L:L + halo, :]
    xa = _silu(acc)

    xh = xa[:, :SSD_WIDTH]
    bm = xa[:, SSD_WIDTH:SSD_WIDTH + SSD_GROUPS * SSD_STATE]
    cm = xa[:, SSD_WIDTH + SSD_GROUPS * SSD_STATE:]

    dtr = dt_ref[0] + dtb_ref[...]
    dt = jnp.maximum(dtr, 0.0) + jnp.log(1.0 + jnp.exp(-jnp.abs(dtr)))
    a = -jnp.exp(alog_ref[...])
    adt = dt * a
    r_i = lax.broadcasted_iota(I32, (L, L), 0)
    c_i = lax.broadcasted_iota(I32, (L, L), 1)
    lower = r_i >= c_i
    tril = jnp.where(lower, 1.0, 0.0)
    acum = jnp.dot(tril, adt, preferred_element_type=F32, precision=lax.Precision.HIGHEST)
    acum_t = acum.T
    dsk = dsk_ref[...]

    nt = (((1,), (1,)), ((), ()))
    rr = SSD_HEADS // SSD_GROUPS
    ys = []
    for g in range(SSD_GROUPS):
        cg = cm[:, g * SSD_STATE:(g + 1) * SSD_STATE]
        bg = bm[:, g * SSD_STATE:(g + 1) * SSD_STATE]
        cgb = cg.astype(BF16)
        bgb = bg.astype(BF16)
        cb = lax.dot_general(cgb, bgb, nt, preferred_element_type=F32)
        bgt = bg.T.astype(BF16)
        for r in range(rr):
            h = g * rr + r
            ac = acum[:, h:h + 1]
            ar = acum_t[h:h + 1, :]
            lmat = jnp.where(lower, jnp.exp(ac - ar), 0.0)
            xhh = xh[:, h * SSD_HEAD_DIM:(h + 1) * SSD_HEAD_DIM]
            xdt = xhh * dt[:, h:h + 1]
            yd = jnp.dot((cb * lmat).astype(BF16), xdt.astype(BF16), preferred_element_type=F32)
            st = st_ref[h]
            yo = jnp.dot(cgb, st.astype(BF16), preferred_element_type=F32) * jnp.exp(ac)
            ys.append(yd + yo + xhh * jnp.broadcast_to(dsk[0:1, h:h + 1], (L, 1)))
            alast = jnp.broadcast_to(acum[L - 1:L, h:h + 1], (L, 1))
            decay = jnp.exp(alast - ac)
            st_ref[h] = jnp.exp(alast) * st + jnp.dot(bgt, (xdt * decay).astype(BF16), preferred_element_type=F32)
    y = jnp.concatenate(ys, axis=1)
    y = y * _silu(z_ref[0])
    o_ref[0] = _rms(y, ng_ref[...]).astype(BF16)


def _ssd(z, xbc, dt, conv_w, conv_b, dt_bias_p, a_log_p, d_skip_p, norm_g):
    b, s, _ = z.shape
    blk = lambda width: pl.BlockSpec((1, SSD_CHUNK, width), lambda bi, i: (bi, i, 0))
    const = lambda r, width: pl.BlockSpec((r, width), lambda bi, i: (0, 0))
    return pl.pallas_call(
        _ssd_kernel,
        grid=(b, s // SSD_CHUNK),
        in_specs=[blk(SSD_WIDTH), blk(CONV_DIM), blk(LANES), const(SSD_CONV, CONV_DIM), const(1, CONV_DIM),
                  const(1, LANES), const(1, LANES), const(1, LANES), const(1, SSD_WIDTH)],
        out_specs=blk(SSD_WIDTH),
        out_shape=jax.ShapeDtypeStruct((b, s, SSD_WIDTH), BF16),
        scratch_shapes=[pltpu.VMEM((SSD_CHUNK + 2 * SUBLANES, CONV_DIM), F32),
                        pltpu.VMEM((SSD_HEADS, SSD_STATE, SSD_HEAD_DIM), F32)],
        compiler_params=_cparams(("arbitrary", "arbitrary")),
        name="ssd",
    )(z, xbc, dt, conv_w, conv_b, dt_bias_p, a_log_p, d_skip_p, norm_g)


def _outproj_kernel(attn_ref, ssd_ref, x_ref, mod_ref, pmg_ref, pfg_ref, wo_ref, wq_ref,
                    x1_ref, xlo_ref, xhi_ref, qp_ref):
    mod = mod_ref[0]
    wo = wo_ref[...]
    mix = (jnp.dot(attn_ref[0], wo[:ATTN_WIDTH], preferred_element_type=F32)
           + jnp.dot(ssd_ref[0], wo[ATTN_WIDTH:], preferred_element_type=F32))
    x1 = x_ref[0] + mod[2:3] * _rms(mix, pmg_ref[...])
    x1_ref[0] = x1
    h2 = _rms(x1, pfg_ref[...]) * (1.0 + mod[4:5]) + mod[3:4]
    half = h2.shape[1] // 2
    xlo_ref[0] = h2[:, :half]
    xhi_ref[0] = h2[:, half:]
    qp_ref[0] = jnp.dot(h2.astype(BF16), wq_ref[...], preferred_element_type=F32).astype(BF16)


def _outproj(attn, ssd, x, mod3, pmg, pfg, wo, wq):
    b, s, d = x.shape
    tm = min(TOK_TILE, s)
    tok = lambda width: pl.BlockSpec((1, tm, width), lambda bi, i: (bi, i, 0))
    const = lambda r, width: pl.BlockSpec((r, width), lambda bi, i: (0, 0))
    return pl.pallas_call(
        _outproj_kernel,
        grid=(b, s // tm),
        in_specs=[tok(ATTN_WIDTH), tok(SSD_WIDTH), tok(d),
                  pl.BlockSpec((1, 6, d), lambda bi, i: (bi, 0, 0)),
                  const(1, d), const(1, d), const(wo.shape[0], d), const(d, wq.shape[1])],
        out_specs=[tok(d), tok(d // 2), tok(d // 2), tok(wq.shape[1])],
        out_shape=[jax.ShapeDtypeStruct((b, s, d), F32),
                   jax.ShapeDtypeStruct((b, s, d // 2), F32),
                   jax.ShapeDtypeStruct((b, s, d // 2), F32),
                   jax.ShapeDtypeStruct((b, s, wq.shape[1]), BF16)],
        compiler_params=_cparams(("arbitrary", "arbitrary")),
        name="outproj",
    )(attn, ssd, x, mod3, pmg, pfg, wo, wq)


_NO_ID = 3.0e38


def _extract_top(vals, ids, n_out, payload=None):
    out_v, out_i, out_p = [], [], []
    for _ in range(n_out):
        m = jnp.max(vals, axis=0, keepdims=True)
        first = jnp.min(jnp.where(vals == m, ids, _NO_ID), axis=0, keepdims=True)
        hit = ids == first
        out_v.append(m)
        out_i.append(first)
        if payload is not None:
            out_p.append(jnp.max(jnp.where(hit, payload, -1.0), axis=0, keepdims=True))
        vals = jnp.where(hit, -jnp.inf, vals)
    cat = lambda rows: jnp.concatenate(rows, axis=0)
    return cat(out_v), cat(out_i), (cat(out_p) if payload is not None else None)


def _candidate_blocks(sv0, sv1, si0, si1):
    k = PEER_TOPK
    t = sv0.shape[1]
    row8 = lax.broadcasted_iota(I32, (SUBLANES, t), 0)
    row8f = row8.astype(F32)
    vals, ids, eids = [], [], []
    r16 = lax.broadcasted_iota(I32, (k, t), 0).astype(F32)
    vals.append(sv0[0:1] + sv1)
    ids.append(r16)
    eids.append(si0[0:1] * PEER_NKEYS + si1)
    for a in range(1, SUBLANES):
        nb = k // (a + 1)
        ok = row8 < nb
        vals.append(jnp.where(ok, sv0[a:a + 1] + sv1[0:SUBLANES], -jnp.inf))
        ids.append(row8f + float(a * k))
        eids.append(si0[a:a + 1] * PEER_NKEYS + si1[0:SUBLANES])
    vals.append(sv0[SUBLANES:k] + sv1[0:1])
    ids.append((row8f + float(SUBLANES)) * float(k))
    eids.append(si0[SUBLANES:k] * PEER_NKEYS + si1[0:1])
    cat = lambda rows: jnp.concatenate(rows, axis=0)
    return cat(vals), cat(ids), cat(eids)


def _route_kernel(qp_ref, keys_ref, eidx_ref, gate_ref):
    qp = qp_ref[...]
    nt = (((1,), (1,)), ((), ()))
    kid = lax.broadcasted_iota(I32, (PEER_NKEYS, qp.shape[0]), 0).astype(F32)
    e_rows = []
    g_rows = []
    for h in range(PEER_HEADS):
        sv = []
        si = []
        for i in range(2):
            o = h * PEER_KEY_DIM + i * PEER_HALF
            st = lax.dot_general(keys_ref[h, i], qp[:, o:o + PEER_HALF], nt, preferred_element_type=F32)
            v, ix, _ = _extract_top(st, kid, PEER_TOPK)
            sv.append(v)
            si.append(ix)
        cand, flat, eid_all = _candidate_blocks(sv[0], sv[1], si[0], si[1])
        fv, _, eid = _extract_top(cand, flat, PEER_TOPK, payload=eid_all)
        p = jnp.exp(fv - fv[0:1, :])
        g_rows.append(p * (1.0 / jnp.sum(p, axis=0, keepdims=True)))
        e_rows.append(eid)
    eid = jnp.concatenate(e_rows, axis=0)
    eidx_ref[0] = (eid.T * float(ROW_WORDS)).astype(I32)
    gate_ref[0] = jnp.concatenate(g_rows, axis=0)


def _route(qp2, keys_b):
    t = qp2.shape[0]
    nb = t // PEER_TB
    slots = PEER_HEADS * PEER_TOPK
    return pl.pallas_call(
        _route_kernel,
        grid=(nb,),
        in_specs=[pl.BlockSpec((PEER_TB, qp2.shape[1]), lambda i: (i, 0)),
                  pl.BlockSpec(keys_b.shape, lambda i: (0, 0, 0, 0))],
        out_specs=[pl.BlockSpec((1, PEER_TB, slots), lambda i: (i, 0, 0)),
                   pl.BlockSpec((1, slots, PEER_TB), lambda i: (i, 0, 0))],
        out_shape=[jax.ShapeDtypeStruct((nb, PEER_TB, slots), I32),
                   jax.ShapeDtypeStruct((nb, slots, PEER_TB), F32)],
        compiler_params=_cparams(("arbitrary",)),
        name="peer_route",
    )(qp2, keys_b)


def _gelu_exact(x):
    return 0.5 * x * (1.0 + lax.erf(x * (2.0 ** -0.5)))


PEER_UNROLL = 16


def _gather_rows(tab_ref, idx_ref, t, g_ref, after=None):
    if after is not None:
        t = t + lax.shift_right_arithmetic(after, 31)
    e4 = None
    for j in range(idx_ref.shape[2]):
        e4 = pl.multiple_of(idx_ref[0, t, j], ROW_WORDS)
        g_ref[j * ROW_WORDS:(j + 1) * ROW_WORDS, :] = tab_ref[pl.ds(e4, ROW_WORDS), :]
    return e4


def _pipelined_tokens(tab_ref, idx_ref, bufs, issue, finish):
    tb = idx_ref.shape[1]
    _gather_rows(tab_ref, idx_ref, 0, bufs[0])

    def step(k, carry):
        t0 = PEER_UNROLL * k
        last = None
        r = issue(bufs[0], t0)
        for u in range(PEER_UNROLL):
            nxt = jnp.minimum(t0 + u + 1, tb - 1)
            last = _gather_rows(tab_ref, idx_ref, nxt, bufs[(u + 1) % 2], after=last)
            finish(r, t0 + u)
            if u + 1 < PEER_UNROLL:
                r = issue(bufs[(u + 1) % 2], t0 + u + 1)
        return carry

    lax.fori_loop(0, tb // PEER_UNROLL, step, 0)


def _chunk_matrix(g_ref, c):
    slots = g_ref.shape[0] // ROW_WORDS
    return pltpu.bitcast(g_ref[pl.ds(c, slots, stride=ROW_WORDS), :], BF16)


def _peer_u_kernel(idx_ref, xa_ref, xb_ref, gate_ref, tab_ref, act_ref, g0_ref, g1_ref, s_ref):
    tb = idx_ref.shape[1]
    slots = idx_ref.shape[2]
    cols = 2 * slots
    r8 = lax.broadcasted_iota(I32, (SUBLANES, LANES), 0)
    even8 = (lax.broadcasted_iota(I32, (SUBLANES, cols), 1) & 1) == 0
    nt = (((1,), (1,)), ((), ()))

    def scores(g_ref, t):
        xa = xa_ref[pl.ds(t, 1), :]
        xb = xb_ref[pl.ds(t, 1), :]
        xa_hi = xa.astype(BF16).astype(F32)
        xb_hi = xb.astype(BF16).astype(F32)
        r = jnp.zeros((SUBLANES, cols), F32)
        for c in range(ROW_WORDS):
            sl = slice(c * LANES, (c + 1) * LANES)
            lhs = jnp.where(r8 == 0, xa_hi[:, sl],
                            jnp.where(r8 == 1, xb_hi[:, sl],
                                      jnp.where(r8 == 2, (xa - xa_hi)[:, sl],
                                                jnp.where(r8 == 3, (xb - xb_hi)[:, sl], 0.0)))).astype(BF16)
            r = r + lax.dot_general(lhs, _chunk_matrix(g_ref, c), nt, preferred_element_type=F32)
        return r

    def finish(r, t):
        tsum = r + pltpu.roll(r, SUBLANES - 2, 0)
        u = jnp.where(even8, tsum, pltpu.roll(tsum, SUBLANES - 1, 0))
        r0 = pl.multiple_of(t * SUBLANES, SUBLANES)
        for h in range(cols // LANES):
            s_ref[h, pl.ds(r0, SUBLANES), :] = u[:, h * LANES:(h + 1) * LANES]

    _pipelined_tokens(tab_ref, idx_ref, (g0_ref, g1_ref), scores, finish)
    part = jnp.concatenate([s_ref[h, pl.ds(0, tb, stride=SUBLANES), :] for h in range(cols // LANES)], axis=1)
    even = (lax.broadcasted_iota(I32, (tb, cols), 1) & 1) == 0
    s = part + jnp.where(even, pltpu.roll(part, cols - 1, 1), pltpu.roll(part, 1, 1))
    r_i = lax.broadcasted_iota(I32, (slots, cols), 0)
    c_i = lax.broadcasted_iota(I32, (slots, cols), 1)
    dup = jnp.where((c_i >> 1) == r_i, 1.0, 0.0)
    gate2 = jnp.dot(gate_ref[0].T, dup, preferred_element_type=F32, precision=lax.Precision.HIGHEST)
    act_ref[...] = _gelu_exact(s) * gate2


def _peer_v_kernel(idx_ref, act_ref, tab_ref, f_ref, g0_ref, g1_ref):
    tb = idx_ref.shape[1]
    slots = idx_ref.shape[2]
    r16 = lax.broadcasted_iota(I32, (2 * SUBLANES, 2 * slots), 0)
    odd_lane = lax.broadcasted_iota(I32, (2 * SUBLANES, 2 * slots), 1) & 1
    keep = [(r16 & (SUBLANES - 1)) == c + ROW_WORDS * odd_lane for c in range(ROW_WORDS)]

    def outputs(g_ref, t):
        a = act_ref[pl.ds(t, 1), :]
        a_hi = a.astype(BF16).astype(F32)
        a2 = jnp.where(r16 < SUBLANES, a_hi, a - a_hi)
        r = jnp.zeros((2 * SUBLANES, LANES), F32)
        for c in range(ROW_WORDS):
            lhs = jnp.where(keep[c], a2, 0.0).astype(BF16)
            r = r + jnp.dot(lhs, _chunk_matrix(g_ref, c), preferred_element_type=F32)
        return r

    def finish(r, t):
        f_ref[t] = r[0:SUBLANES] + r[SUBLANES:2 * SUBLANES]

    _pipelined_tokens(tab_ref, idx_ref, (g0_ref, g1_ref), outputs, finish)


def _table_spec(tab):
    return pl.BlockSpec(tab.shape, lambda i: (0, 0), pipeline_mode=pl.Buffered(1))


def _gather_scratch(slots):
    return [pltpu.VMEM((slots * ROW_WORDS, LANES), I32)] * 2


def _peer_u(eidx, xa, xb, gate, tab):
    nb, tb, slots = eidx.shape
    smem = pl.BlockSpec((1, tb, slots), lambda i: (i, 0, 0), memory_space=pltpu.SMEM)
    xrow = pl.BlockSpec((tb, xa.shape[1]), lambda i: (i, 0))
    return pl.pallas_call(
        _peer_u_kernel,
        grid=(nb,),
        in_specs=[smem, xrow, xrow, pl.BlockSpec((1, slots, tb), lambda i: (i, 0, 0)), _table_spec(tab)],
        out_specs=pl.BlockSpec((tb, 2 * slots), lambda i: (i, 0)),
        out_shape=jax.ShapeDtypeStruct((nb * tb, 2 * slots), F32),
        scratch_shapes=_gather_scratch(slots) + [pltpu.VMEM((2 * slots // LANES, tb * SUBLANES, LANES), F32)],
        compiler_params=_cparams(("arbitrary",)),
        name="peer_u",
    )(eidx, xa, xb, gate, tab)


def _peer_v(eidx, act, tab, d):
    nb, tb, slots = eidx.shape
    smem = pl.BlockSpec((1, tb, slots), lambda i: (i, 0, 0), memory_space=pltpu.SMEM)
    return pl.pallas_call(
        _peer_v_kernel,
        grid=(nb,),
        in_specs=[smem, pl.BlockSpec((tb, 2 * slots), lambda i: (i, 0)), _table_spec(tab)],
        out_specs=pl.BlockSpec((tb, d // LANES, LANES), lambda i: (i, 0, 0)),
        out_shape=jax.ShapeDtypeStruct((nb * tb, d // LANES, LANES), F32),
        scratch_shapes=_gather_scratch(slots),
        compiler_params=_cparams(("arbitrary",)),
        name="peer_v",
    )(eidx, act, tab)


def _pack_table(w):
    e, d = w.shape
    halves = w.astype(BF16).reshape(e, 2, ROW_WORDS, LANES)
    pairs = jnp.moveaxis(halves, 1, -1)
    words = lax.bitcast_convert_type(pairs, jnp.uint32)
    return lax.bitcast_convert_type(words, I32).reshape(e * ROW_WORDS, LANES)


def _final_kernel(x1_ref, f_ref, mod_ref, g_ref, o_ref):
    o_ref[0] = x1_ref[0] + mod_ref[0][5:6] * _rms(f_ref[0], g_ref[...])


def _final(x1, f, mod3, g):
    b, s, d = x1.shape
    tm = min(TOK_TILE, s)
    tok = lambda width: pl.BlockSpec((1, tm, width), lambda bi, i: (bi, i, 0))
    return pl.pallas_call(
        _final_kernel,
        grid=(b, s // tm),
        in_specs=[tok(d), tok(d),
                  pl.BlockSpec((1, 6, d), lambda bi, i: (bi, 0, 0)),
                  pl.BlockSpec((1, d), lambda bi, i: (0, 0))],
        out_specs=tok(d),
        out_shape=jax.ShapeDtypeStruct((b, s, d), F32),
        compiler_params=_cparams(("arbitrary", "arbitrary")),
        name="final",
    )(x1, f, mod3, g)


def _pad_in_weights(w_in):
    cols = []
    src = 0
    for _, real, pad in _SEG:
        blk = w_in[:, src:src + real]
        if pad > real:
            blk = jnp.pad(blk, ((0, 0), (0, pad - real)))
        cols.append(blk)
        src += real
    return jnp.concatenate(cols, axis=1).astype(BF16)


def _pad_lanes(v):
    return jnp.pad(v.reshape(1, -1), ((0, 0), (0, LANES - v.shape[-1])))


def kernel(x, c, positions, w_ada, b_ada, pre_mix_g, post_mix_g, w_in, conv_w, conv_b, dt_bias, a_log,
           d_skip, ssd_norm_g, w_out, pre_ffn_g, post_ffn_g, peer_w_q, peer_sub_keys, peer_u, peer_v):
    b, s, d = x.shape
    depth = w_ada.shape[0]
    inv_freq = ROPE_THETA ** (-jnp.arange(0, HEAD_DIM, 2, dtype=F32) / HEAD_DIM)
    invf = jnp.tile(inv_freq, LANES // inv_freq.shape[0]).reshape(1, LANES)
    pos3 = positions.astype(F32)[..., None]
    for l in range(depth):
        mod3 = _adaln(c, w_ada[l], b_ada[l]).reshape(b, 6, d)
        q, k, v, qi, ki, wi, z, xbc, dt = _inproj(
            x, mod3, pre_mix_g[l].reshape(1, d), pos3, invf, _pad_in_weights(w_in[l]))
        attn = _dsa(qi, ki, wi, q, k, v)
        ssd = _ssd(z, xbc, dt, conv_w[l], conv_b[l].reshape(1, -1), _pad_lanes(dt_bias[l]), _pad_lanes(a_log[l]),
                   _pad_lanes(d_skip[l]), ssd_norm_g[l].reshape(1, -1))
        x1, xlo, xhi, qp = _outproj(attn, ssd, x, mod3, post_mix_g[l].reshape(1, d), pre_ffn_g[l].reshape(1, d),
                                    w_out[l].astype(BF16), peer_w_q[l].astype(BF16))
        t = b * s
        eidx, gate = _route(qp.reshape(t, -1), peer_sub_keys[l].astype(BF16))
        act = _peer_u(eidx, xlo.reshape(t, d // 2), xhi.reshape(t, d // 2), gate, _pack_table(peer_u[l]))
        f = _peer_v(eidx, act, _pack_table(peer_v[l]), d)
        x = _final(x1, f.reshape(b, s, d), mod3, post_ffn_g[l].reshape(1, d))
    return x
```

```python
import functools
import math

import jax
import jax.numpy as jnp
import numpy as np
from jax import lax
from jax.experimental import pallas as pl
from jax.experimental.pallas import tpu as pltpu

F32 = jnp.float32
BF16 = jnp.bfloat16
I32 = jnp.int32

ATTN_HEADS = 8
ATTN_KV_HEADS = 2
HEAD_DIM = 64
ATTN_WIDTH = ATTN_HEADS * HEAD_DIM
KV_WIDTH = ATTN_KV_HEADS * HEAD_DIM
IDX_HEADS = 8
IDX_DIM = 64
IDX_TOPK_MAX = 256
Q_BLOCK = 128
SSD_HEADS = 8
SSD_HEAD_DIM = 64
SSD_WIDTH = SSD_HEADS * SSD_HEAD_DIM
SSD_GROUPS = 2
SSD_STATE = 128
SSD_CONV = 4
SSD_CHUNK = 128
CONV_DIM = SSD_WIDTH + 2 * SSD_GROUPS * SSD_STATE
PEER_HEADS = 8
PEER_NKEYS = 128
PEER_KEY_DIM = 128
PEER_HALF = PEER_KEY_DIM // 2
PEER_TOPK = 16
ROPE_THETA = 10000.0
NORM_EPS = 1e-6

LANES = 128
SUBLANES = 8
VMEM_LIMIT = 56 * 1024 * 1024

_SEG = (("q", ATTN_WIDTH, ATTN_WIDTH), ("k", KV_WIDTH, KV_WIDTH), ("v", KV_WIDTH, KV_WIDTH),
        ("qi", IDX_HEADS * IDX_DIM, IDX_HEADS * IDX_DIM), ("ki", IDX_DIM, LANES), ("wi", IDX_HEADS, LANES),
        ("z", SSD_WIDTH, SSD_WIDTH), ("xbc", CONV_DIM, CONV_DIM), ("dt", SSD_HEADS, LANES))
_OFF = {}
_o = 0
for _n, _real, _pad in _SEG:
    _OFF[_n] = (_o, _real, _pad)
    _o += _pad
PROJ_PAD = _o

TOK_TILE = 256
PEER_TB = 128
ROW_WORDS = 4


def _cparams(sem):
    return pltpu.CompilerParams(dimension_semantics=sem, vmem_limit_bytes=VMEM_LIMIT)


def _rms(x, g):
    return x * lax.rsqrt(jnp.mean(x * x, axis=-1, keepdims=True) + NORM_EPS) * g


def _silu(x):
    return x * (1.0 / (1.0 + jnp.exp(-x)))


def _adaln_kernel(c_ref, w_ref, b_ref, o_ref):
    a = _silu(c_ref[...])
    o_ref[...] = jnp.dot(a, w_ref[...], preferred_element_type=F32,
                         precision=lax.Precision.HIGHEST) + b_ref[...]


def _adaln(c, w, b):
    bsz, d = c.shape
    n = w.shape[1]
    tn = 1024
    return pl.pallas_call(
        _adaln_kernel,
        grid=(n // tn,),
        in_specs=[pl.BlockSpec((bsz, d), lambda j: (0, 0)),
                  pl.BlockSpec((d, tn), lambda j: (0, j)),
                  pl.BlockSpec((1, tn), lambda j: (0, j))],
        out_specs=pl.BlockSpec((bsz, tn), lambda j: (0, j)),
        out_shape=jax.ShapeDtypeStruct((bsz, n), F32),
        compiler_params=_cparams(("arbitrary",)),
        name="adaln",
    )(c, w, b.reshape(1, n))


def _inproj_kernel(x_ref, mod_ref, g_ref, pos_ref, invf_ref, w_ref,
                   q_ref, k_ref, v_ref, qi_ref, ki_ref, wi_ref, z_ref, xbc_ref, dt_ref):
    x = x_ref[0]
    mod = mod_ref[0]
    h = _rms(x, g_ref[...]) * (1.0 + mod[1:2]) + mod[0:1]
    proj = jnp.dot(h.astype(BF16), w_ref[...], preferred_element_type=F32)

    tm = x.shape[0]
    ang = pos_ref[0] * invf_ref[...]
    cos = jnp.cos(ang)
    sin = jnp.sin(ang)
    lane = lax.broadcasted_iota(I32, (tm, LANES), 1)
    first = (lane & (HEAD_DIM // 2)) == 0
    sin_s = jnp.where(first, -sin, sin)

    def rope(xs):
        r = jnp.where(first, pltpu.roll(xs, LANES - HEAD_DIM // 2, 1), pltpu.roll(xs, HEAD_DIM // 2, 1))
        return xs * cos + r * sin_s

    def seg(name):
        o, _, p = _OFF[name]
        return proj[:, o:o + p]

    def rope_seg(name, scale):
        s = seg(name)
        parts = [rope(s[:, j * LANES:(j + 1) * LANES]) * scale for j in range(s.shape[1] // LANES)]
        return parts[0] if len(parts) == 1 else jnp.concatenate(parts, axis=1)

    q_ref[0] = rope_seg("q", HEAD_DIM ** -0.5).astype(BF16)
    k_ref[0] = rope_seg("k", 1.0).astype(BF16)
    v_ref[0] = seg("v").astype(BF16)
    qi_ref[0] = rope_seg("qi", IDX_DIM ** -0.5).astype(BF16)
    ki_ref[0] = rope_seg("ki", 1.0).astype(BF16)
    wi_ref[0] = seg("wi") * (IDX_HEADS ** -0.5)
    z_ref[0] = seg("z")
    xbc_ref[0] = seg("xbc")
    dt_ref[0] = seg("dt")


def _inproj(x, mod3, g, pos3, invf, w_pad):
    b, s, d = x.shape
    tm = min(TOK_TILE, s)
    tok = lambda width: pl.BlockSpec((1, tm, width), lambda bi, i: (bi, i, 0))
    outs = (("q", BF16), ("k", BF16), ("v", BF16), ("qi", BF16), ("ki", BF16),
            ("wi", F32), ("z", F32), ("xbc", F32), ("dt", F32))
    return pl.pallas_call(
        _inproj_kernel,
        grid=(b, s // tm),
        in_specs=[tok(d),
                  pl.BlockSpec((1, 6, d), lambda bi, i: (bi, 0, 0)),
                  pl.BlockSpec((1, d), lambda bi, i: (0, 0)),
                  tok(1),
                  pl.BlockSpec((1, LANES), lambda bi, i: (0, 0)),
                  pl.BlockSpec((d, PROJ_PAD), lambda bi, i: (0, 0))],
        out_specs=[tok(_OFF[n][2]) for n, _ in outs],
        out_shape=[jax.ShapeDtypeStruct((b, s, _OFF[n][2]), dt) for n, dt in outs],
        compiler_params=_cparams(("arbitrary", "arbitrary")),
        name="inproj",
    )(x, mod3, g, pos3, invf, w_pad)


_INT_MIN = -2 ** 31
_KEY_NEG_INF = 0x807FFFFF - 2 ** 32


DSA_KEY_SPAN = 512
DSA_ROWS = 256
DSA_REFINE = 6


def _dsa_kernel(qi_ref, ki_ref, wi_ref, q_ref, k_ref, v_ref, o_ref, *, topk):
    i = pl.program_id(1)
    span = min(DSA_KEY_SPAN, ki_ref.shape[1])
    n_span = ki_ref.shape[1] // span
    need_spans = ((i + 1) * qi_ref.shape[1] - 1) // span + 1
    for c in range(1, n_span + 1):
        @pl.when(need_spans == c)
        def _():
            _dsa_block(qi_ref, ki_ref, wi_ref, q_ref, k_ref, v_ref, o_ref, topk=topk, s_len=c * span)


def _dsa_block(qi_ref, ki_ref, wi_ref, q_ref, k_ref, v_ref, o_ref, *, topk, s_len):
    i = pl.program_id(1)
    nq = qi_ref.shape[1]
    qi = qi_ref[0]
    ki = ki_ref[0, 0:s_len, 0:IDX_DIM]
    w = wi_ref[0]
    nt = (((1,), (1,)), ((), ()))

    score = jnp.zeros((nq, s_len), F32)
    for h in range(IDX_HEADS):
        lg = lax.dot_general(qi[:, h * IDX_DIM:(h + 1) * IDX_DIM], ki, nt, preferred_element_type=F32)
        score = score + w[:, h:h + 1] * jnp.maximum(lg, 0.0)

    qpos = i * nq + lax.broadcasted_iota(I32, (nq, s_len), 0)
    kpos = lax.broadcasted_iota(I32, (nq, s_len), 1)
    causal = kpos <= qpos
    score = jnp.where(causal, score, -jnp.inf)

    kf = float(topk)

    def count(mask):
        return jnp.sum(jnp.where(mask, 1.0, 0.0), axis=1, keepdims=True)

    def threshold(u):
        key = jnp.maximum(u ^ jnp.int32(_INT_MIN), jnp.int32(_KEY_NEG_INF))
        return lax.bitcast_convert_type(jnp.where(key < 0, key ^ jnp.int32(0x7FFFFFFF), key), F32)

    def bs_body(it, ans):
        cand = ans | jnp.left_shift(jnp.int32(1), 31 - it)
        return jnp.where(count(score >= threshold(cand)) >= kf, cand, ans)

    searched = (i + 1) * nq > topk
    code = lax.fori_loop(0, jnp.where(searched, 32, 0), bs_body, jnp.zeros((nq, 1), I32))

    def refine(_, lh):
        lo, hi = lh
        mid = lo + 0.5 * (hi - lo)
        ok = count(score >= mid) >= kf
        return jnp.where(ok, mid, lo), jnp.where(ok, hi, mid)

    thr, _ = lax.fori_loop(0, jnp.where(searched, DSA_REFINE, 0), refine, (threshold(code), threshold(code + 1)))

    gt = score > thr
    eq = score == thr
    need = kf - count(gt)
    eqb = jnp.where(eq, 1.0, 0.0).astype(BF16)
    r_i = lax.broadcasted_iota(I32, (LANES, LANES), 0)
    c_i = lax.broadcasted_iota(I32, (LANES, LANES), 1)
    tri = jnp.where(r_i <= c_i, 1.0, 0.0).astype(BF16)
    ones = jnp.ones((LANES, LANES), BF16)
    carry = jnp.zeros((nq, LANES), F32)
    prefix = []
    for c in range(s_len // LANES):
        ec = eqb[:, c * LANES:(c + 1) * LANES]
        prefix.append(jnp.dot(ec, tri, preferred_element_type=F32) + carry)
        carry = carry + jnp.dot(ec, ones, preferred_element_type=F32)
    keep = jnp.concatenate(prefix, axis=1) <= need
    sel = jnp.logical_and(causal, jnp.logical_or(gt, jnp.logical_and(eq, keep)))
    bias = jnp.where(sel, 0.0, -jnp.inf)

    q = q_ref[0]
    k = k_ref[0, 0:s_len, :]
    v = v_ref[0, 0:s_len, :]
    grp = ATTN_HEADS // ATTN_KV_HEADS
    outs = []
    for h in range(ATTN_HEADS):
        g = h // grp
        kg = k[:, g * HEAD_DIM:(g + 1) * HEAD_DIM]
        vg = v[:, g * HEAD_DIM:(g + 1) * HEAD_DIM]
        sc = lax.dot_general(q[:, h * HEAD_DIM:(h + 1) * HEAD_DIM], kg, nt, preferred_element_type=F32) + bias
        m = jnp.max(sc, axis=1, keepdims=True)
        p = jnp.exp(sc - m)
        l = jnp.sum(p, axis=1, keepdims=True)
        o = jnp.dot(p.astype(BF16), vg, preferred_element_type=F32)
        outs.append(o * (1.0 / l))
    o_ref[0] = jnp.concatenate(outs, axis=1).astype(BF16)


def _dsa(qi, ki, wi, q, k, v):
    b, s, _ = q.shape
    topk = min(IDX_TOPK_MAX, s // 4)
    rows = min(DSA_ROWS, s)
    blk = lambda width: pl.BlockSpec((1, rows, width), lambda bi, i: (bi, i, 0))
    full = lambda width: pl.BlockSpec((1, s, width), lambda bi, i: (bi, 0, 0))
    return pl.pallas_call(
        functools.partial(_dsa_kernel, topk=topk),
        grid=(b, s // rows),
        in_specs=[blk(IDX_HEADS * IDX_DIM), full(LANES), blk(LANES), blk(ATTN_WIDTH), full(KV_WIDTH), full(KV_WIDTH)],
        out_specs=blk(ATTN_WIDTH),
        out_shape=jax.ShapeDtypeStruct((b, s, ATTN_WIDTH), BF16),
        compiler_params=_cparams(("arbitrary", "arbitrary")),
        name="dsa",
    )(qi, ki, wi, q, k, v)


def _ssd_kernel(z_ref, xbc_ref, dt_ref, cw_ref, cb_ref, dtb_ref, alog_ref, dsk_ref, ng_ref,
                o_ref, xs_ref, st_ref):
    c = pl.program_id(1)
    L = SSD_CHUNK
    halo = SUBLANES

    @pl.when(c == 0)
    def _():
        xs_ref[0:halo, :] = jnp.zeros((halo, CONV_DIM), F32)
        st_ref[...] = jnp.zeros(st_ref.shape, F32)

    xs_ref[halo:halo + L, :] = xbc_ref[0]
    cw = cw_ref[...]
    acc = cb_ref[...]
    for j in range(SSD_CONV):
        o = halo - (SSD_CONV - 1) + j
        acc = acc + xs_ref[o:o + L, :] * cw[j:j + 1, :]
    xs_ref[0:halo, :] = xs_ref[L:L + halo, :]
    xa = _silu(acc)

    xh = xa[:, :SSD_WIDTH]
    bm = xa[:, SSD_WIDTH:SSD_WIDTH + SSD_GROUPS * SSD_STATE]
    cm = xa[:, SSD_WIDTH + SSD_GROUPS * SSD_STATE:]

    dtr = dt_ref[0] + dtb_ref[...]
    dt = jnp.maximum(dtr, 0.0) + jnp.log(1.0 + jnp.exp(-jnp.abs(dtr)))
    a = -jnp.exp(alog_ref[...])
    adt = dt * a
    r_i = lax.broadcasted_iota(I32, (L, L), 0)
    c_i = lax.broadcasted_iota(I32, (L, L), 1)
    lower = r_i >= c_i
    tril = jnp.where(lower, 1.0, 0.0)
    acum = jnp.dot(tril, adt, preferred_element_type=F32, precision=lax.Precision.HIGHEST)
    acum_t = acum.T
    dsk = dsk_ref[...]

    nt = (((1,), (1,)), ((), ()))
    rr = SSD_HEADS // SSD_GROUPS
    ys = []
    for g in range(SSD_GROUPS):
        cg = cm[:, g * SSD_STATE:(g + 1) * SSD_STATE]
        bg = bm[:, g * SSD_STATE:(g + 1) * SSD_STATE]
        cgb = cg.astype(BF16)
        bgb = bg.astype(BF16)
        cb = lax.dot_general(cgb, bgb, nt, preferred_element_type=F32)
        bgt = bg.T.astype(BF16)
        for r in range(rr):
            h = g * rr + r
            ac = acum[:, h:h + 1]
            ar = acum_t[h:h + 1, :]
            lmat = jnp.where(lower, jnp.exp(ac - ar), 0.0)
            xhh = xh[:, h * SSD_HEAD_DIM:(h + 1) * SSD_HEAD_DIM]
            xdt = xhh * dt[:, h:h + 1]
            yd = jnp.dot((cb * lmat).astype(BF16), xdt.astype(BF16), preferred_element_type=F32)
            st = st_ref[h]
            yo = jnp.dot(cgb, st.astype(BF16), preferred_element_type=F32) * jnp.exp(ac)
            ys.append(yd + yo + xhh * jnp.broadcast_to(dsk[0:1, h:h + 1], (L, 1)))
            alast = jnp.broadcast_to(acum[L - 1:L, h:h + 1], (L, 1))
            decay = jnp.exp(alast - ac)
            st_ref[h] = jnp.exp(alast) * st + jnp.dot(bgt, (xdt * decay).astype(BF16), preferred_element_type=F32)
    y = jnp.concatenate(ys, axis=1)
    y = y * _silu(z_ref[0])
    o_ref[0] = _rms(y, ng_ref[...]).astype(BF16)


def _ssd(z, xbc, dt, conv_w, conv_b, dt_bias_p, a_log_p, d_skip_p, norm_g):
    b, s, _ = z.shape
    blk = lambda width: pl.BlockSpec((1, SSD_CHUNK, width), lambda bi, i: (bi, i, 0))
    const = lambda r, width: pl.BlockSpec((r, width), lambda bi, i: (0, 0))
    return pl.pallas_call(
        _ssd_kernel,
        grid=(b, s // SSD_CHUNK),
        in_specs=[blk(SSD_WIDTH), blk(CONV_DIM), blk(LANES), const(SSD_CONV, CONV_DIM), const(1, CONV_DIM),
                  const(1, LANES), const(1, LANES), const(1, LANES), const(1, SSD_WIDTH)],
        out_specs=blk(SSD_WIDTH),
        out_shape=jax.ShapeDtypeStruct((b, s, SSD_WIDTH), BF16),
        scratch_shapes=[pltpu.VMEM((SSD_CHUNK + 2 * SUBLANES, CONV_DIM), F32),
                        pltpu.VMEM((SSD_HEADS, SSD_STATE, SSD_HEAD_DIM), F32)],
        compiler_params=_cparams(("arbitrary", "arbitrary")),
        name="ssd",
    )(z, xbc, dt, conv_w, conv_b, dt_bias_p, a_log_p, d_skip_p, norm_g)


def _outproj_kernel(attn_ref, ssd_ref, x_ref, mod_ref, pmg_ref, pfg_ref, wo_ref, wq_ref,
                    x1_ref, xlo_ref, xhi_ref, qp_ref):
    mod = mod_ref[0]
    wo = wo_ref[...]
    mix = (jnp.dot(attn_ref[0], wo[:ATTN_WIDTH], preferred_element_type=F32)
           + jnp.dot(ssd_ref[0], wo[ATTN_WIDTH:], preferred_element_type=F32))
    x1 = x_ref[0] + mod[2:3] * _rms(mix, pmg_ref[...])
    x1_ref[0] = x1
    h2 = _rms(x1, pfg_ref[...]) * (1.0 + mod[4:5]) + mod[3:4]
    half = h2.shape[1] // 2
    xlo_ref[...] = h2[:, :half]
    xhi_ref[...] = h2[:, half:]
    qp_ref[...] = jnp.dot(h2.astype(BF16), wq_ref[...], preferred_element_type=F32).astype(BF16)


def _outproj(attn, ssd, x, mod3, pmg, pfg, wo, wq):
    b, s, d = x.shape
    tm = min(TOK_TILE, s)
    steps = s // tm
    tok = lambda width: pl.BlockSpec((1, tm, width), lambda bi, i: (bi, i, 0))
    flat = lambda width: pl.BlockSpec((tm, width), lambda bi, i: (bi * steps + i, 0))
    const = lambda r, width: pl.BlockSpec((r, width), lambda bi, i: (0, 0))
    return pl.pallas_call(
        _outproj_kernel,
        grid=(b, steps),
        in_specs=[tok(ATTN_WIDTH), tok(SSD_WIDTH), tok(d),
                  pl.BlockSpec((1, 6, d), lambda bi, i: (bi, 0, 0)),
                  const(1, d), const(1, d), const(wo.shape[0], d), const(d, wq.shape[1])],
        out_specs=[tok(d), flat(d // 2), flat(d // 2), flat(wq.shape[1])],
        out_shape=[jax.ShapeDtypeStruct((b, s, d), F32),
                   jax.ShapeDtypeStruct((b * s, d // 2), F32),
                   jax.ShapeDtypeStruct((b * s, d // 2), F32),
                   jax.ShapeDtypeStruct((b * s, wq.shape[1]), BF16)],
        compiler_params=_cparams(("arbitrary", "arbitrary")),
        name="outproj",
    )(attn, ssd, x, mod3, pmg, pfg, wo, wq)


_NO_ID = 3.0e38


def _extract_top(vals, ids, n_out, payload=None):
    out_v, out_i, out_p = [], [], []
    for _ in range(n_out):
        m = jnp.max(vals, axis=0, keepdims=True)
        first = jnp.min(jnp.where(vals == m, ids, _NO_ID), axis=0, keepdims=True)
        hit = ids == first
        out_v.append(m)
        out_i.append(first)
        if payload is not None:
            out_p.append(jnp.max(jnp.where(hit, payload, -1.0), axis=0, keepdims=True))
        vals = jnp.where(hit, -jnp.inf, vals)
    cat = lambda rows: jnp.concatenate(rows, axis=0)
    return cat(out_v), cat(out_i), (cat(out_p) if payload is not None else None)


def _candidate_blocks(sv0, sv1, si0, si1):
    k = PEER_TOPK
    t = sv0.shape[1]
    row8 = lax.broadcasted_iota(I32, (SUBLANES, t), 0)
    row8f = row8.astype(F32)
    vals, ids, eids = [], [], []
    r16 = lax.broadcasted_iota(I32, (k, t), 0).astype(F32)
    vals.append(sv0[0:1] + sv1)
    ids.append(r16)
    eids.append(si0[0:1] * PEER_NKEYS + si1)
    for a in range(1, SUBLANES):
        nb = k // (a + 1)
        ok = row8 < nb
        vals.append(jnp.where(ok, sv0[a:a + 1] + sv1[0:SUBLANES], -jnp.inf))
        ids.append(row8f + float(a * k))
        eids.append(si0[a:a + 1] * PEER_NKEYS + si1[0:SUBLANES])
    vals.append(sv0[SUBLANES:k] + sv1[0:1])
    ids.append((row8f + float(SUBLANES)) * float(k))
    eids.append(si0[SUBLANES:k] * PEER_NKEYS + si1[0:1])
    cat = lambda rows: jnp.concatenate(rows, axis=0)
    return cat(vals), cat(ids), cat(eids)


def _route_kernel(qp_ref, keys_ref, eidx_ref, gate_ref):
    qp = qp_ref[...]
    nt = (((1,), (1,)), ((), ()))
    kid = lax.broadcasted_iota(I32, (PEER_NKEYS, qp.shape[0]), 0).astype(F32)
    e_rows = []
    g_rows = []
    for h in range(PEER_HEADS):
        sv = []
        si = []
        for i in range(2):
            o = h * PEER_KEY_DIM + i * PEER_HALF
            st = lax.dot_general(keys_ref[h, i], qp[:, o:o + PEER_HALF], nt, preferred_element_type=F32)
            v, ix, _ = _extract_top(st, kid, PEER_TOPK)
            sv.append(v)
            si.append(ix)
        cand, flat, eid_all = _candidate_blocks(sv[0], sv[1], si[0], si[1])
        fv, _, eid = _extract_top(cand, flat, PEER_TOPK, payload=eid_all)
        p = jnp.exp(fv - fv[0:1, :])
        g_rows.append(p * (1.0 / jnp.sum(p, axis=0, keepdims=True)))
        e_rows.append(eid)
    eid = jnp.concatenate(e_rows, axis=0)
    eidx_ref[0] = (eid.T * float(ROW_WORDS)).astype(I32)
    gate_ref[0] = jnp.concatenate(g_rows, axis=0)


def _route(qp2, keys_b):
    t = qp2.shape[0]
    nb = t // PEER_TB
    slots = PEER_HEADS * PEER_TOPK
    return pl.pallas_call(
        _route_kernel,
        grid=(nb,),
        in_specs=[pl.BlockSpec((PEER_TB, qp2.shape[1]), lambda i: (i, 0)),
                  pl.BlockSpec(keys_b.shape, lambda i: (0, 0, 0, 0))],
        out_specs=[pl.BlockSpec((1, PEER_TB, slots), lambda i: (i, 0, 0)),
                   pl.BlockSpec((1, slots, PEER_TB), lambda i: (i, 0, 0))],
        out_shape=[jax.ShapeDtypeStruct((nb, PEER_TB, slots), I32),
                   jax.ShapeDtypeStruct((nb, slots, PEER_TB), F32)],
        compiler_params=_cparams(("arbitrary",)),
        name="peer_route",
    )(qp2, keys_b)


def _gelu_exact(x):
    return 0.5 * x * (1.0 + lax.erf(x * (2.0 ** -0.5)))


PEER_UNROLL = 32


def _gather_rows(tab_ref, idx_ref, t, g_ref, after=None):
    if after is not None:
        t = t + lax.shift_right_arithmetic(after, 31)
    e4 = None
    for j in range(idx_ref.shape[2]):
        e4 = pl.multiple_of(idx_ref[0, t, j], ROW_WORDS)
        g_ref[j * ROW_WORDS:(j + 1) * ROW_WORDS, :] = tab_ref[pl.ds(e4, ROW_WORDS), :]
    return e4


def _pipelined_tokens(tab_ref, idx_ref, bufs, issue, finish):
    tb = idx_ref.shape[1]
    _gather_rows(tab_ref, idx_ref, 0, bufs[0])

    def step(k, carry):
        t0 = PEER_UNROLL * k
        last = None
        r = issue(bufs[0], t0)
        for u in range(PEER_UNROLL):
            nxt = min(t0 + u + 1, tb - 1) if isinstance(t0, int) else jnp.minimum(t0 + u + 1, tb - 1)
            last = _gather_rows(tab_ref, idx_ref, nxt, bufs[(u + 1) % 2], after=None if isinstance(t0, int) else last)
            finish(r, t0 + u)
            if u + 1 < PEER_UNROLL:
                r = issue(bufs[(u + 1) % 2], t0 + u + 1)
        return carry

    if tb == PEER_UNROLL:
        step(0, 0)
    else:
        lax.fori_loop(0, tb // PEER_UNROLL, step, 0)


def _chunk_matrix(g_ref, c):
    slots = g_ref.shape[0] // ROW_WORDS
    return pltpu.bitcast(g_ref[pl.ds(c, slots, stride=ROW_WORDS), :], BF16)


def _peer_u_kernel(tab_ref, idx_ref, xa_ref, xb_ref, gate_ref, act_ref, g0_ref, g1_ref, s_ref):
    tb = idx_ref.shape[1]
    slots = idx_ref.shape[2]
    cols = 2 * slots
    r8 = lax.broadcasted_iota(I32, (SUBLANES, LANES), 0)
    even8 = (lax.broadcasted_iota(I32, (SUBLANES, cols), 1) & 1) == 0
    nt = (((1,), (1,)), ((), ()))

    def scores(g_ref, t):
        xa = xa_ref[pl.ds(t, 1), :]
        xb = xb_ref[pl.ds(t, 1), :]
        xa_hi = xa.astype(BF16).astype(F32)
        xb_hi = xb.astype(BF16).astype(F32)
        r = jnp.zeros((SUBLANES, cols), F32)
        for c in range(ROW_WORDS):
            sl = slice(c * LANES, (c + 1) * LANES)
            lhs = jnp.where(r8 == 0, xa_hi[:, sl],
                            jnp.where(r8 == 1, xb_hi[:, sl],
                                      jnp.where(r8 == 2, (xa - xa_hi)[:, sl],
                                                jnp.where(r8 == 3, (xb - xb_hi)[:, sl], 0.0)))).astype(BF16)
            r = r + lax.dot_general(lhs, _chunk_matrix(g_ref, c), nt, preferred_element_type=F32)
        return r

    def finish(r, t):
        tsum = r + pltpu.roll(r, SUBLANES - 2, 0)
        u = jnp.where(even8, tsum, pltpu.roll(tsum, SUBLANES - 1, 0))
        r0 = pl.multiple_of(t * SUBLANES, SUBLANES)
        for h in range(cols // LANES):
            s_ref[h, pl.ds(r0, SUBLANES), :] = u[:, h * LANES:(h + 1) * LANES]

    _pipelined_tokens(tab_ref, idx_ref, (g0_ref, g1_ref), scores, finish)
    part = jnp.concatenate([s_ref[h, pl.ds(0, tb, stride=SUBLANES), :] for h in range(cols // LANES)], axis=1)
    even = (lax.broadcasted_iota(I32, (tb, cols), 1) & 1) == 0
    s = part + jnp.where(even, pltpu.roll(part, cols - 1, 1), pltpu.roll(part, 1, 1))
    r_i = lax.broadcasted_iota(I32, (slots, cols), 0)
    c_i = lax.broadcasted_iota(I32, (slots, cols), 1)
    dup = jnp.where((c_i >> 1) == r_i, 1.0, 0.0)
    gate2 = jnp.dot(gate_ref[0].T, dup, preferred_element_type=F32, precision=lax.Precision.HIGHEST)
    act_ref[...] = _gelu_exact(s) * gate2


def _peer_v_kernel(tab_ref, idx_ref, act_ref, f_ref, g0_ref, g1_ref, o_ref):
    tb = idx_ref.shape[1]
    slots = idx_ref.shape[2]
    r16 = lax.broadcasted_iota(I32, (2 * SUBLANES, 2 * slots), 0)
    odd_lane = lax.broadcasted_iota(I32, (2 * SUBLANES, 2 * slots), 1) & 1
    keep = [(r16 & (SUBLANES - 1)) == c + ROW_WORDS * odd_lane for c in range(ROW_WORDS)]

    def outputs(g_ref, t):
        a = act_ref[pl.ds(t, 1), :]
        a_hi = a.astype(BF16).astype(F32)
        a2 = jnp.where(r16 < SUBLANES, a_hi, a - a_hi)
        r = jnp.zeros((2 * SUBLANES, LANES), F32)
        for c in range(ROW_WORDS):
            lhs = jnp.where(keep[c], a2, 0.0).astype(BF16)
            r = r + jnp.dot(lhs, _chunk_matrix(g_ref, c), preferred_element_type=F32)
        return r

    def finish(r, t):
        r0 = pl.multiple_of(t * SUBLANES, SUBLANES)
        o_ref[pl.ds(r0, SUBLANES), :] = r[0:SUBLANES] + r[SUBLANES:2 * SUBLANES]

    _pipelined_tokens(tab_ref, idx_ref, (g0_ref, g1_ref), outputs, finish)
    for c in range(f_ref.shape[1] // LANES):
        f_ref[:, c * LANES:(c + 1) * LANES] = o_ref[pl.ds(c, tb, stride=SUBLANES), :]


def _table_spec(tab):
    return pl.BlockSpec(tab.shape, lambda i: (0, 0), pipeline_mode=pl.Buffered(1))


def _gather_scratch(slots):
    return [pltpu.VMEM((slots * ROW_WORDS, LANES), I32)] * 2


def _peer_u(eidx, xa, xb, gate, tab):
    nb, tb, slots = eidx.shape
    smem = pl.BlockSpec((1, tb, slots), lambda i: (i, 0, 0), memory_space=pltpu.SMEM)
    xrow = pl.BlockSpec((tb, xa.shape[1]), lambda i: (i, 0))
    return pl.pallas_call(
        _peer_u_kernel,
        grid=(nb,),
        in_specs=[_table_spec(tab), smem, xrow, xrow, pl.BlockSpec((1, slots, tb), lambda i: (i, 0, 0))],
        out_specs=pl.BlockSpec((tb, 2 * slots), lambda i: (i, 0)),
        out_shape=jax.ShapeDtypeStruct((nb * tb, 2 * slots), F32),
        scratch_shapes=_gather_scratch(slots) + [pltpu.VMEM((2 * slots // LANES, tb * SUBLANES, LANES), F32)],
        compiler_params=_cparams(("arbitrary",)),
        name="peer_u",
    )(tab, eidx, xa, xb, gate)


def _peer_v(eidx, act, tab, d):
    nb, tb, slots = eidx.shape
    smem = pl.BlockSpec((1, tb, slots), lambda i: (i, 0, 0), memory_space=pltpu.SMEM)
    return pl.pallas_call(
        _peer_v_kernel,
        grid=(nb,),
        in_specs=[_table_spec(tab), smem, pl.BlockSpec((tb, 2 * slots), lambda i: (i, 0))],
        out_specs=pl.BlockSpec((tb, d), lambda i: (i, 0)),
        out_shape=jax.ShapeDtypeStruct((nb * tb, d), F32),
        scratch_shapes=_gather_scratch(slots) + [pltpu.VMEM((tb * d // LANES, LANES), F32)],
        compiler_params=_cparams(("arbitrary",)),
        name="peer_v",
    )(tab, eidx, act)


def _pack_table(w):
    e, d = w.shape
    halves = w.astype(BF16).reshape(e, 2, ROW_WORDS, LANES)
    pairs = jnp.moveaxis(halves, 1, -1)
    words = lax.bitcast_convert_type(pairs, jnp.uint32)
    return lax.bitcast_convert_type(words, I32).reshape(e * ROW_WORDS, LANES)


def _final_kernel(x1_ref, f_ref, mod_ref, g_ref, o_ref):
    o_ref[0] = x1_ref[0] + mod_ref[0][5:6] * _rms(f_ref[...], g_ref[...])


def _final(x1, f, mod3, g):
    b, s, d = x1.shape
    tm = min(TOK_TILE, s)
    steps = s // tm
    tok = lambda width: pl.BlockSpec((1, tm, width), lambda bi, i: (bi, i, 0))
    return pl.pallas_call(
        _final_kernel,
        grid=(b, steps),
        in_specs=[tok(d), pl.BlockSpec((tm, d), lambda bi, i: (bi * steps + i, 0)),
                  pl.BlockSpec((1, 6, d), lambda bi, i: (bi, 0, 0)),
                  pl.BlockSpec((1, d), lambda bi, i: (0, 0))],
        out_specs=tok(d),
        out_shape=jax.ShapeDtypeStruct((b, s, d), F32),
        compiler_params=_cparams(("arbitrary", "arbitrary")),
        name="final",
    )(x1, f, mod3, g)


def _pad_in_weights(w_in):
    cols = []
    src = 0
    for _, real, pad in _SEG:
        blk = w_in[:, src:src + real]
        if pad > real:
            blk = jnp.pad(blk, ((0, 0), (0, pad - real)))
        cols.append(blk)
        src += real
    return jnp.concatenate(cols, axis=1).astype(BF16)


def _pad_lanes(v):
    return jnp.pad(v.reshape(1, -1), ((0, 0), (0, LANES - v.shape[-1])))


def kernel(x, c, positions, w_ada, b_ada, pre_mix_g, post_mix_g, w_in, conv_w, conv_b, dt_bias, a_log,
           d_skip, ssd_norm_g, w_out, pre_ffn_g, post_ffn_g, peer_w_q, peer_sub_keys, peer_u, peer_v):
    b, s, d = x.shape
    depth = w_ada.shape[0]
    inv_freq = ROPE_THETA ** (-jnp.arange(0, HEAD_DIM, 2, dtype=F32) / HEAD_DIM)
    invf = jnp.tile(inv_freq, LANES // inv_freq.shape[0]).reshape(1, LANES)
    pos3 = positions.astype(F32)[..., None]
    for l in range(depth):
        mod3 = _adaln(c, w_ada[l], b_ada[l]).reshape(b, 6, d)
        q, k, v, qi, ki, wi, z, xbc, dt = _inproj(
            x, mod3, pre_mix_g[l].reshape(1, d), pos3, invf, _pad_in_weights(w_in[l]))
        attn = _dsa(qi, ki, wi, q, k, v)
        ssd = _ssd(z, xbc, dt, conv_w[l], conv_b[l].reshape(1, -1), _pad_lanes(dt_bias[l]), _pad_lanes(a_log[l]),
                   _pad_lanes(d_skip[l]), ssd_norm_g[l].reshape(1, -1))
        x1, xlo, xhi, qp = _outproj(attn, ssd, x, mod3, post_mix_g[l].reshape(1, d), pre_ffn_g[l].reshape(1, d),
                                    w_out[l].astype(BF16), peer_w_q[l].astype(BF16))
        eidx, gate = _route(qp, peer_sub_keys[l].astype(BF16))
        act = _peer_u(eidx, xlo, xhi, gate, _pack_table(peer_u[l]))
        f = _peer_v(eidx, act, _pack_table(peer_v[l]), d)
        x = _final(x1, f, mod3, post_ffn_g[l].reshape(1, d))
    return x
```

```python
import functools
import math

import jax
import jax.numpy as jnp
import numpy as np
from jax import lax
from jax.experimental import pallas as pl
from jax.experimental.pallas import tpu as pltpu

F32 = jnp.float32
BF16 = jnp.bfloat16
I32 = jnp.int32

ATTN_HEADS = 8
ATTN_KV_HEADS = 2
HEAD_DIM = 64
ATTN_WIDTH = ATTN_HEADS * HEAD_DIM
KV_WIDTH = ATTN_KV_HEADS * HEAD_DIM
IDX_HEADS = 8
IDX_DIM = 64
IDX_TOPK_MAX = 256
Q_BLOCK = 128
SSD_HEADS = 8
SSD_HEAD_DIM = 64
SSD_WIDTH = SSD_HEADS * SSD_HEAD_DIM
SSD_GROUPS = 2
SSD_STATE = 128
SSD_CONV = 4
SSD_CHUNK = 128
CONV_DIM = SSD_WIDTH + 2 * SSD_GROUPS * SSD_STATE
PEER_HEADS = 8
PEER_NKEYS = 128
PEER_KEY_DIM = 128
PEER_HALF = PEER_KEY_DIM // 2
PEER_TOPK = 16
ROPE_THETA = 10000.0
NORM_EPS = 1e-6

LANES = 128
SUBLANES = 8
VMEM_LIMIT = 56 * 1024 * 1024

_SEG = (("q", ATTN_WIDTH, ATTN_WIDTH), ("k", KV_WIDTH, KV_WIDTH), ("v", KV_WIDTH, KV_WIDTH),
        ("qi", IDX_HEADS * IDX_DIM, IDX_HEADS * IDX_DIM), ("ki", IDX_DIM, LANES), ("wi", IDX_HEADS, LANES),
        ("z", SSD_WIDTH, SSD_WIDTH), ("xbc", CONV_DIM, CONV_DIM), ("dt", SSD_HEADS, LANES))
_OFF = {}
_o = 0
for _n, _real, _pad in _SEG:
    _OFF[_n] = (_o, _real, _pad)
    _o += _pad
PROJ_PAD = _o

TOK_TILE = 256
PEER_TB = 128
ROW_WORDS = 4


def _cparams(sem):
    return pltpu.CompilerParams(dimension_semantics=sem, vmem_limit_bytes=VMEM_LIMIT)


def _rms(x, g):
    return x * lax.rsqrt(jnp.mean(x * x, axis=-1, keepdims=True) + NORM_EPS) * g


def _silu(x):
    return x * (1.0 / (1.0 + jnp.exp(-x)))


def _adaln_kernel(c_ref, w_ref, b_ref, o_ref):
    a = _silu(c_ref[...])
    o_ref[...] = jnp.dot(a, w_ref[...], preferred_element_type=F32,
                         precision=lax.Precision.HIGHEST) + b_ref[...]


def _adaln(c, w, b):
    bsz, d = c.shape
    n = w.shape[1]
    tn = 1024
    return pl.pallas_call(
        _adaln_kernel,
        grid=(n // tn,),
        in_specs=[pl.BlockSpec((bsz, d), lambda j: (0, 0)),
                  pl.BlockSpec((d, tn), lambda j: (0, j)),
                  pl.BlockSpec((1, tn), lambda j: (0, j))],
        out_specs=pl.BlockSpec((bsz, tn), lambda j: (0, j)),
        out_shape=jax.ShapeDtypeStruct((bsz, n), F32),
        compiler_params=_cparams(("arbitrary",)),
        name="adaln",
    )(c, w, b.reshape(1, n))


def _inproj_kernel(x_ref, mod_ref, g_ref, pos_ref, invf_ref, w_ref,
                   q_ref, k_ref, v_ref, qi_ref, ki_ref, wi_ref, z_ref, xbc_ref, dt_ref):
    x = x_ref[0]
    mod = mod_ref[0]
    h = _rms(x, g_ref[...]) * (1.0 + mod[1:2]) + mod[0:1]
    proj = jnp.dot(h.astype(BF16), w_ref[...], preferred_element_type=F32)

    tm = x.shape[0]
    ang = pos_ref[0] * invf_ref[...]
    cos = jnp.cos(ang)
    sin = jnp.sin(ang)
    lane = lax.broadcasted_iota(I32, (tm, LANES), 1)
    first = (lane & (HEAD_DIM // 2)) == 0
    sin_s = jnp.where(first, -sin, sin)

    def rope(xs):
        r = jnp.where(first, pltpu.roll(xs, LANES - HEAD_DIM // 2, 1), pltpu.roll(xs, HEAD_DIM // 2, 1))
        return xs * cos + r * sin_s

    def seg(name):
        o, _, p = _OFF[name]
        return proj[:, o:o + p]

    def rope_seg(name, scale):
        s = seg(name)
        parts = [rope(s[:, j * LANES:(j + 1) * LANES]) * scale for j in range(s.shape[1] // LANES)]
        return parts[0] if len(parts) == 1 else jnp.concatenate(parts, axis=1)

    q_ref[0] = rope_seg("q", HEAD_DIM ** -0.5).astype(BF16)
    k_ref[0] = rope_seg("k", 1.0).astype(BF16)
    v_ref[0] = seg("v").astype(BF16)
    qi_ref[0] = rope_seg("qi", IDX_DIM ** -0.5).astype(BF16)
    ki_ref[0] = rope_seg("ki", 1.0).astype(BF16)
    wi_ref[0] = seg("wi") * (IDX_HEADS ** -0.5)
    z_ref[0] = seg("z")
    xbc_ref[0] = seg("xbc")
    dt_ref[0] = seg("dt")


def _inproj(x, mod3, g, pos3, invf, w_pad):
    b, s, d = x.shape
    tm = min(TOK_TILE, s)
    tok = lambda width: pl.BlockSpec((1, tm, width), lambda bi, i: (bi, i, 0))
    outs = (("q", BF16), ("k", BF16), ("v", BF16), ("qi", BF16), ("ki", BF16),
            ("wi", F32), ("z", F32), ("xbc", F32), ("dt", F32))
    return pl.pallas_call(
        _inproj_kernel,
        grid=(b, s // tm),
        in_specs=[tok(d),
                  pl.BlockSpec((1, 6, d), lambda bi, i: (bi, 0, 0)),
                  pl.BlockSpec((1, d), lambda bi, i: (0, 0)),
                  tok(1),
                  pl.BlockSpec((1, LANES), lambda bi, i: (0, 0)),
                  pl.BlockSpec((d, PROJ_PAD), lambda bi, i: (0, 0))],
        out_specs=[tok(_OFF[n][2]) for n, _ in outs],
        out_shape=[jax.ShapeDtypeStruct((b, s, _OFF[n][2]), dt) for n, dt in outs],
        compiler_params=_cparams(("arbitrary", "arbitrary")),
        name="inproj",
    )(x, mod3, g, pos3, invf, w_pad)


_INT_MIN = -2 ** 31
_KEY_NEG_INF = 0x807FFFFF - 2 ** 32


DSA_KEY_SPAN = 512
DSA_ROWS = 256
DSA_REFINE = 6


def _dsa_kernel(qi_ref, ki_ref, wi_ref, q_ref, k_ref, v_ref, o_ref, *, topk):
    i = pl.program_id(1)
    span = min(DSA_KEY_SPAN, ki_ref.shape[1])
    n_span = ki_ref.shape[1] // span
    need_spans = ((i + 1) * qi_ref.shape[1] - 1) // span + 1
    for c in range(1, n_span + 1):
        @pl.when(need_spans == c)
        def _():
            _dsa_block(qi_ref, ki_ref, wi_ref, q_ref, k_ref, v_ref, o_ref, topk=topk, s_len=c * span)


def _dsa_block(qi_ref, ki_ref, wi_ref, q_ref, k_ref, v_ref, o_ref, *, topk, s_len):
    i = pl.program_id(1)
    nq = qi_ref.shape[1]
    qi = qi_ref[0]
    ki = ki_ref[0, 0:s_len, 0:IDX_DIM]
    w = wi_ref[0]
    nt = (((1,), (1,)), ((), ()))

    score = jnp.zeros((nq, s_len), F32)
    for h in range(IDX_HEADS):
        lg = lax.dot_general(qi[:, h * IDX_DIM:(h + 1) * IDX_DIM], ki, nt, preferred_element_type=F32)
        score = score + w[:, h:h + 1] * jnp.maximum(lg, 0.0)

    qpos = i * nq + lax.broadcasted_iota(I32, (nq, s_len), 0)
    kpos = lax.broadcasted_iota(I32, (nq, s_len), 1)
    causal = kpos <= qpos
    score = jnp.where(causal, score, -jnp.inf)

    kf = float(topk)

    def count(mask):
        return jnp.sum(jnp.where(mask, 1.0, 0.0), axis=1, keepdims=True)

    def threshold(u):
        key = jnp.maximum(u ^ jnp.int32(_INT_MIN), jnp.int32(_KEY_NEG_INF))
        return lax.bitcast_convert_type(jnp.where(key < 0, key ^ jnp.int32(0x7FFFFFFF), key), F32)

    def bs_body(it, ans):
        cand = ans | jnp.left_shift(jnp.int32(1), 31 - it)
        return jnp.where(count(score >= threshold(cand)) >= kf, cand, ans)

    searched = (i + 1) * nq > topk
    code = lax.fori_loop(0, jnp.where(searched, 32, 0), bs_body, jnp.zeros((nq, 1), I32))

    def refine(_, lh):
        lo, hi = lh
        mid = lo + 0.5 * (hi - lo)
        ok = count(score >= mid) >= kf
        return jnp.where(ok, mid, lo), jnp.where(ok, hi, mid)

    thr, _ = lax.fori_loop(0, jnp.where(searched, DSA_REFINE, 0), refine, (threshold(code), threshold(code + 1)))

    gt = score > thr
    eq = score == thr
    need = kf - count(gt)
    eqb = jnp.where(eq, 1.0, 0.0).astype(BF16)
    r_i = lax.broadcasted_iota(I32, (LANES, LANES), 0)
    c_i = lax.broadcasted_iota(I32, (LANES, LANES), 1)
    tri = jnp.where(r_i <= c_i, 1.0, 0.0).astype(BF16)
    ones = jnp.ones((LANES, LANES), BF16)
    carry = jnp.zeros((nq, LANES), F32)
    prefix = []
    for c in range(s_len // LANES):
        ec = eqb[:, c * LANES:(c + 1) * LANES]
        prefix.append(jnp.dot(ec, tri, preferred_element_type=F32) + carry)
        carry = carry + jnp.dot(ec, ones, preferred_element_type=F32)
    keep = jnp.concatenate(prefix, axis=1) <= need
    sel = jnp.logical_and(causal, jnp.logical_or(gt, jnp.logical_and(eq, keep)))
    bias = jnp.where(sel, 0.0, -jnp.inf)

    q = q_ref[0]
    k = k_ref[0, 0:s_len, :]
    v = v_ref[0, 0:s_len, :]
    grp = ATTN_HEADS // ATTN_KV_HEADS
    outs = []
    for h in range(ATTN_HEADS):
        g = h // grp
        kg = k[:, g * HEAD_DIM:(g + 1) * HEAD_DIM]
        vg = v[:, g * HEAD_DIM:(g + 1) * HEAD_DIM]
        sc = lax.dot_general(q[:, h * HEAD_DIM:(h + 1) * HEAD_DIM], kg, nt, preferred_element_type=F32) + bias
        m = jnp.max(sc, axis=1, keepdims=True)
        p = jnp.exp(sc - m)
        l = jnp.sum(p, axis=1, keepdims=True)
        o = jnp.dot(p.astype(BF16), vg, preferred_element_type=F32)
        outs.append(o * (1.0 / l))
    o_ref[0] = jnp.concatenate(outs, axis=1).astype(BF16)


def _dsa(qi, ki, wi, q, k, v):
    b, s, _ = q.shape
    topk = min(IDX_TOPK_MAX, s // 4)
    rows = min(DSA_ROWS, s)
    blk = lambda width: pl.BlockSpec((1, rows, width), lambda bi, i: (bi, i, 0))
    full = lambda width: pl.BlockSpec((1, s, width), lambda bi, i: (bi, 0, 0))
    return pl.pallas_call(
        functools.partial(_dsa_kernel, topk=topk),
        grid=(b, s // rows),
        in_specs=[blk(IDX_HEADS * IDX_DIM), full(LANES), blk(LANES), blk(ATTN_WIDTH), full(KV_WIDTH), full(KV_WIDTH)],
        out_specs=blk(ATTN_WIDTH),
        out_shape=jax.ShapeDtypeStruct((b, s, ATTN_WIDTH), BF16),
        compiler_params=_cparams(("arbitrary", "arbitrary")),
        name="dsa",
    )(qi, ki, wi, q, k, v)


def _ssd_kernel(z_ref, xbc_ref, dt_ref, cw_ref, cb_ref, dtb_ref, alog_ref, dsk_ref, ng_ref,
                o_ref, xs_ref, st_ref):
    c = pl.program_id(1)
    L = SSD_CHUNK
    halo = SUBLANES

    @pl.when(c == 0)
    def _():
        xs_ref[0:halo, :] = jnp.zeros((halo, CONV_DIM), F32)
        st_ref[...] = jnp.zeros(st_ref.shape, F32)

    xs_ref[halo:halo + L, :] = xbc_ref[0]
    cw = cw_ref[...]
    acc = cb_ref[...]
    for j in range(SSD_CONV):
        o = halo - (SSD_CONV - 1) + j
        acc = acc + xs_ref[o:o + L, :] * cw[j:j + 1, :]
    xs_ref[0:halo, :] = xs_ref[L:L + halo, :]
    xa = _silu(acc)

    xh = xa[:, :SSD_WIDTH]
    bm = xa[:, SSD_WIDTH:SSD_WIDTH + SSD_GROUPS * SSD_STATE]
    cm = xa[:, SSD_WIDTH + SSD_GROUPS * SSD_STATE:]

    dtr = dt_ref[0] + dtb_ref[...]
    dt = jnp.maximum(dtr, 0.0) + jnp.log(1.0 + jnp.exp(-jnp.abs(dtr)))
    a = -jnp.exp(alog_ref[...])
    adt = dt * a
    r_i = lax.broadcasted_iota(I32, (L, L), 0)
    c_i = lax.broadcasted_iota(I32, (L, L), 1)
    lower = r_i >= c_i
    tril = jnp.where(lower, 1.0, 0.0)
    acum = jnp.dot(tril, adt, preferred_element_type=F32, precision=lax.Precision.HIGHEST)
    acum_t = acum.T
    dsk = dsk_ref[...]

    nt = (((1,), (1,)), ((), ()))
    rr = SSD_HEADS // SSD_GROUPS
    ys = []
    for g in range(SSD_GROUPS):
        cg = cm[:, g * SSD_STATE:(g + 1) * SSD_STATE]
        bg = bm[:, g * SSD_STATE:(g + 1) * SSD_STATE]
        cgb = cg.astype(BF16)
        bgb = bg.astype(BF16)
        cb = lax.dot_general(cgb, bgb, nt, preferred_element_type=F32)
        bgt = bg.T.astype(BF16)
        for r in range(rr):
            h = g * rr + r
            ac = acum[:, h:h + 1]
            ar = acum_t[h:h + 1, :]
            lmat = jnp.where(lower, jnp.exp(ac - ar), 0.0)
            xhh = xh[:, h * SSD_HEAD_DIM:(h + 1) * SSD_HEAD_DIM]
            xdt = xhh * dt[:, h:h + 1]
            yd = jnp.dot((cb * lmat).astype(BF16), xdt.astype(BF16), preferred_element_type=F32)
            st = st_ref[h]
            yo = jnp.dot(cgb, st.astype(BF16), preferred_element_type=F32) * jnp.exp(ac)
            ys.append(yd + yo + xhh * jnp.broadcast_to(dsk[0:1, h:h + 1], (L, 1)))
            alast = jnp.broadcast_to(acum[L - 1:L, h:h + 1], (L, 1))
            decay = jnp.exp(alast - ac)
            st_ref[h] = jnp.exp(alast) * st + jnp.dot(bgt, (xdt * decay).astype(BF16), preferred_element_type=F32)
    y = jnp.concatenate(ys, axis=1)
    y = y * _silu(z_ref[0])
    o_ref[0] = _rms(y, ng_ref[...]).astype(BF16)


def _ssd(z, xbc, dt, conv_w, conv_b, dt_bias_p, a_log_p, d_skip_p, norm_g):
    b, s, _ = z.shape
    blk = lambda width: pl.BlockSpec((1, SSD_CHUNK, width), lambda bi, i: (bi, i, 0))
    const = lambda r, width: pl.BlockSpec((r, width), lambda bi, i: (0, 0))
    return pl.pallas_call(
        _ssd_kernel,
        grid=(b, s // SSD_CHUNK),
        in_specs=[blk(SSD_WIDTH), blk(CONV_DIM), blk(LANES), const(SSD_CONV, CONV_DIM), const(1, CONV_DIM),
                  const(1, LANES), const(1, LANES), const(1, LANES), const(1, SSD_WIDTH)],
        out_specs=blk(SSD_WIDTH),
        out_shape=jax.ShapeDtypeStruct((b, s, SSD_WIDTH), BF16),
        scratch_shapes=[pltpu.VMEM((SSD_CHUNK + 2 * SUBLANES, CONV_DIM), F32),
                        pltpu.VMEM((SSD_HEADS, SSD_STATE, SSD_HEAD_DIM), F32)],
        compiler_params=_cparams(("arbitrary", "arbitrary")),
        name="ssd",
    )(z, xbc, dt, conv_w, conv_b, dt_bias_p, a_log_p, d_skip_p, norm_g)


def _outproj_kernel(attn_ref, ssd_ref, x_ref, mod_ref, pmg_ref, pfg_ref, wo_ref, wq_ref,
                    x1_ref, xlo_ref, xhi_ref, qp_ref):
    mod = mod_ref[0]
    wo = wo_ref[...]
    mix = (jnp.dot(attn_ref[0], wo[:ATTN_WIDTH], preferred_element_type=F32)
           + jnp.dot(ssd_ref[0], wo[ATTN_WIDTH:], preferred_element_type=F32))
    x1 = x_ref[0] + mod[2:3] * _rms(mix, pmg_ref[...])
    x1_ref[0] = x1
    h2 = _rms(x1, pfg_ref[...]) * (1.0 + mod[4:5]) + mod[3:4]
    half = h2.shape[1] // 2
    xlo_ref[...] = h2[:, :half]
    xhi_ref[...] = h2[:, half:]
    qp_ref[...] = jnp.dot(h2.astype(BF16), wq_ref[...], preferred_element_type=F32).astype(BF16)


def _outproj(attn, ssd, x, mod3, pmg, pfg, wo, wq):
    b, s, d = x.shape
    tm = min(TOK_TILE, s)
    steps = s // tm
    tok = lambda width: pl.BlockSpec((1, tm, width), lambda bi, i: (bi, i, 0))
    flat = lambda width: pl.BlockSpec((tm, width), lambda bi, i: (bi * steps + i, 0))
    const = lambda r, width: pl.BlockSpec((r, width), lambda bi, i: (0, 0))
    return pl.pallas_call(
        _outproj_kernel,
        grid=(b, steps),
        in_specs=[tok(ATTN_WIDTH), tok(SSD_WIDTH), tok(d),
                  pl.BlockSpec((1, 6, d), lambda bi, i: (bi, 0, 0)),
                  const(1, d), const(1, d), const(wo.shape[0], d), const(d, wq.shape[1])],
        out_specs=[tok(d), flat(d // 2), flat(d // 2), flat(wq.shape[1])],
        out_shape=[jax.ShapeDtypeStruct((b, s, d), F32),
                   jax.ShapeDtypeStruct((b * s, d // 2), F32),
                   jax.ShapeDtypeStruct((b * s, d // 2), F32),
                   jax.ShapeDtypeStruct((b * s, wq.shape[1]), BF16)],
        compiler_params=_cparams(("arbitrary", "arbitrary")),
        name="outproj",
    )(attn, ssd, x, mod3, pmg, pfg, wo, wq)


_NO_ID = 3.0e38


def _extract_top(vals, ids, n_out, payload=None):
    out_v, out_i, out_p = [], [], []
    for _ in range(n_out):
        m = jnp.max(vals, axis=0, keepdims=True)
        first = jnp.min(jnp.where(vals == m, ids, _NO_ID), axis=0, keepdims=True)
        hit = ids == first
        out_v.append(m)
        out_i.append(first)
        if payload is not None:
            out_p.append(jnp.max(jnp.where(hit, payload, -1.0), axis=0, keepdims=True))
        vals = jnp.where(hit, -jnp.inf, vals)
    cat = lambda rows: jnp.concatenate(rows, axis=0)
    return cat(out_v), cat(out_i), (cat(out_p) if payload is not None else None)


def _candidate_blocks(sv0, sv1, si0, si1):
    k = PEER_TOPK
    t = sv0.shape[1]
    row8 = lax.broadcasted_iota(I32, (SUBLANES, t), 0)
    row8f = row8.astype(F32)
    vals, ids, eids = [], [], []
    r16 = lax.broadcasted_iota(I32, (k, t), 0).astype(F32)
    vals.append(sv0[0:1] + sv1)
    ids.append(r16)
    eids.append(si0[0:1] * PEER_NKEYS + si1)
    for a in range(1, SUBLANES):
        nb = k // (a + 1)
        ok = row8 < nb
        vals.append(jnp.where(ok, sv0[a:a + 1] + sv1[0:SUBLANES], -jnp.inf))
        ids.append(row8f + float(a * k))
        eids.append(si0[a:a + 1] * PEER_NKEYS + si1[0:SUBLANES])
    vals.append(sv0[SUBLANES:k] + sv1[0:1])
    ids.append((row8f + float(SUBLANES)) * float(k))
    eids.append(si0[SUBLANES:k] * PEER_NKEYS + si1[0:1])
    cat = lambda rows: jnp.concatenate(rows, axis=0)
    return cat(vals), cat(ids), cat(eids)


def _route_kernel(qp_ref, keys_ref, eidx_ref, gate_ref):
    qp = qp_ref[...]
    nt = (((1,), (1,)), ((), ()))
    kid = lax.broadcasted_iota(I32, (PEER_NKEYS, qp.shape[0]), 0).astype(F32)
    e_rows = []
    g_rows = []
    for h in range(PEER_HEADS):
        sv = []
        si = []
        for i in range(2):
            o = h * PEER_KEY_DIM + i * PEER_HALF
            st = lax.dot_general(keys_ref[h, i], qp[:, o:o + PEER_HALF], nt, preferred_element_type=F32)
            v, ix, _ = _extract_top(st, kid, PEER_TOPK)
            sv.append(v)
            si.append(ix)
        cand, flat, eid_all = _candidate_blocks(sv[0], sv[1], si[0], si[1])
        fv, _, eid = _extract_top(cand, flat, PEER_TOPK, payload=eid_all)
        p = jnp.exp(fv - fv[0:1, :])
        g_rows.append(p * (1.0 / jnp.sum(p, axis=0, keepdims=True)))
        e_rows.append(eid)
    eid = jnp.concatenate(e_rows, axis=0)
    eidx_ref[0] = (eid.T * float(ROW_WORDS)).astype(I32)
    gate_ref[0] = jnp.concatenate(g_rows, axis=0)


def _route(qp2, keys_b):
    t = qp2.shape[0]
    nb = t // PEER_TB
    slots = PEER_HEADS * PEER_TOPK
    return pl.pallas_call(
        _route_kernel,
        grid=(nb,),
        in_specs=[pl.BlockSpec((PEER_TB, qp2.shape[1]), lambda i: (i, 0)),
                  pl.BlockSpec(keys_b.shape, lambda i: (0, 0, 0, 0))],
        out_specs=[pl.BlockSpec((1, PEER_TB, slots), lambda i: (i, 0, 0)),
                   pl.BlockSpec((1, slots, PEER_TB), lambda i: (i, 0, 0))],
        out_shape=[jax.ShapeDtypeStruct((nb, PEER_TB, slots), I32),
                   jax.ShapeDtypeStruct((nb, slots, PEER_TB), F32)],
        compiler_params=_cparams(("arbitrary",)),
        name="peer_route",
    )(qp2, keys_b)


def _gelu_exact(x):
    return 0.5 * x * (1.0 + lax.erf(x * (2.0 ** -0.5)))


PEER_UNROLL = 32


def _gather_rows(tab_ref, idx_ref, t, g_ref, after=None):
    if after is not None:
        t = t + lax.shift_right_arithmetic(after, 31)
    e4 = None
    for j in range(idx_ref.shape[2]):
        e4 = pl.multiple_of(idx_ref[0, t, j], ROW_WORDS)
        g_ref[j * ROW_WORDS:(j + 1) * ROW_WORDS, :] = tab_ref[pl.ds(e4, ROW_WORDS), :]
    return e4


def _pipelined_tokens(tab_ref, idx_ref, bufs, issue, finish):
    tb = idx_ref.shape[1]
    _gather_rows(tab_ref, idx_ref, 0, bufs[0])

    def step(k, carry):
        t0 = PEER_UNROLL * k
        last = None
        r = issue(bufs[0], t0)
        for u in range(PEER_UNROLL):
            nxt = min(t0 + u + 1, tb - 1) if isinstance(t0, int) else jnp.minimum(t0 + u + 1, tb - 1)
            last = _gather_rows(tab_ref, idx_ref, nxt, bufs[(u + 1) % 2], after=None if isinstance(t0, int) else last)
            finish(r, t0 + u)
            if u + 1 < PEER_UNROLL:
                r = issue(bufs[(u + 1) % 2], t0 + u + 1)
        return carry

    if tb == PEER_UNROLL:
        step(0, 0)
    else:
        lax.fori_loop(0, tb // PEER_UNROLL, step, 0)


def _chunk_matrix(g_ref, c):
    slots = g_ref.shape[0] // ROW_WORDS
    return pltpu.bitcast(g_ref[pl.ds(c, slots, stride=ROW_WORDS), :], BF16)


def _peer_u_kernel(tab_ref, idx_ref, xa_ref, xb_ref, gate_ref, act_ref, g0_ref, g1_ref, s_ref):
    tb = idx_ref.shape[1]
    slots = idx_ref.shape[2]
    cols = 2 * slots
    r8 = lax.broadcasted_iota(I32, (SUBLANES, LANES), 0)
    even8 = (lax.broadcasted_iota(I32, (SUBLANES, cols), 1) & 1) == 0
    nt = (((1,), (1,)), ((), ()))

    def scores(g_ref, t):
        xa = xa_ref[pl.ds(t, 1), :]
        xb = xb_ref[pl.ds(t, 1), :]
        xa_hi = xa.astype(BF16).astype(F32)
        xb_hi = xb.astype(BF16).astype(F32)
        r = jnp.zeros((SUBLANES, cols), F32)
        for c in range(ROW_WORDS):
            sl = slice(c * LANES, (c + 1) * LANES)
            lhs = jnp.where(r8 == 0, xa_hi[:, sl],
                            jnp.where(r8 == 1, xb_hi[:, sl],
                                      jnp.where(r8 == 2, (xa - xa_hi)[:, sl],
                                                jnp.where(r8 == 3, (xb - xb_hi)[:, sl], 0.0)))).astype(BF16)
            r = r + lax.dot_general(lhs, _chunk_matrix(g_ref, c), nt, preferred_element_type=F32)
        return r

    def finish(r, t):
        tsum = r + pltpu.roll(r, SUBLANES - 2, 0)
        u = jnp.where(even8, tsum, pltpu.roll(tsum, SUBLANES - 1, 0))
        r0 = pl.multiple_of(t * SUBLANES, SUBLANES)
        for h in range(cols // LANES):
            s_ref[h, pl.ds(r0, SUBLANES), :] = u[:, h * LANES:(h + 1) * LANES]

    _pipelined_tokens(tab_ref, idx_ref, (g0_ref, g1_ref), scores, finish)
    part = jnp.concatenate([s_ref[h, pl.ds(0, tb, stride=SUBLANES), :] for h in range(cols // LANES)], axis=1)
    even = (lax.broadcasted_iota(I32, (tb, cols), 1) & 1) == 0
    s = part + jnp.where(even, pltpu.roll(part, cols - 1, 1), pltpu.roll(part, 1, 1))
    r_i = lax.broadcasted_iota(I32, (slots, cols), 0)
    c_i = lax.broadcasted_iota(I32, (slots, cols), 1)
    dup = jnp.where((c_i >> 1) == r_i, 1.0, 0.0)
    gate2 = jnp.dot(gate_ref[0].T, dup, preferred_element_type=F32, precision=lax.Precision.HIGHEST)
    act_ref[...] = _gelu_exact(s) * gate2


def _peer_v_kernel(tab_ref, idx_ref, act_ref, x1_ref, mod_ref, gain_ref, out_ref, g0_ref, g1_ref, o_ref):
    tb = idx_ref.shape[1]
    slots = idx_ref.shape[2]
    r16 = lax.broadcasted_iota(I32, (2 * SUBLANES, 2 * slots), 0)
    odd_lane = lax.broadcasted_iota(I32, (2 * SUBLANES, 2 * slots), 1) & 1
    keep = [(r16 & (SUBLANES - 1)) == c + ROW_WORDS * odd_lane for c in range(ROW_WORDS)]

    def outputs(g_ref, t):
        a = act_ref[pl.ds(t, 1), :]
        a_hi = a.astype(BF16).astype(F32)
        a2 = jnp.where(r16 < SUBLANES, a_hi, a - a_hi)
        r = jnp.zeros((2 * SUBLANES, LANES), F32)
        for c in range(ROW_WORDS):
            lhs = jnp.where(keep[c], a2, 0.0).astype(BF16)
            r = r + jnp.dot(lhs, _chunk_matrix(g_ref, c), preferred_element_type=F32)
        return r

    def finish(r, t):
        r0 = pl.multiple_of(t * SUBLANES, SUBLANES)
        o_ref[pl.ds(r0, SUBLANES), :] = r[0:SUBLANES] + r[SUBLANES:2 * SUBLANES]

    _pipelined_tokens(tab_ref, idx_ref, (g0_ref, g1_ref), outputs, finish)
    f = jnp.concatenate([o_ref[pl.ds(c, tb, stride=SUBLANES), :] for c in range(out_ref.shape[2] // LANES)], axis=1)
    out_ref[0] = x1_ref[0] + mod_ref[0][5:6] * _rms(f, gain_ref[...])


def _table_spec(tab):
    return pl.BlockSpec(tab.shape, lambda i: (0, 0), pipeline_mode=pl.Buffered(1))


def _gather_scratch(slots):
    return [pltpu.VMEM((slots * ROW_WORDS, LANES), I32)] * 2


def _peer_u(eidx, xa, xb, gate, tab):
    nb, tb, slots = eidx.shape
    smem = pl.BlockSpec((1, tb, slots), lambda i: (i, 0, 0), memory_space=pltpu.SMEM)
    xrow = pl.BlockSpec((tb, xa.shape[1]), lambda i: (i, 0))
    return pl.pallas_call(
        _peer_u_kernel,
        grid=(nb,),
        in_specs=[_table_spec(tab), smem, xrow, xrow, pl.BlockSpec((1, slots, tb), lambda i: (i, 0, 0))],
        out_specs=pl.BlockSpec((tb, 2 * slots), lambda i: (i, 0)),
        out_shape=jax.ShapeDtypeStruct((nb * tb, 2 * slots), F32),
        scratch_shapes=_gather_scratch(slots) + [pltpu.VMEM((2 * slots // LANES, tb * SUBLANES, LANES), F32)],
        compiler_params=_cparams(("arbitrary",)),
        name="peer_u",
    )(tab, eidx, xa, xb, gate)


def _peer_v(eidx, act, tab, x1, mod3, gain):
    nb, tb, slots = eidx.shape
    b, s, d = x1.shape
    per_b = s // tb
    smem = pl.BlockSpec((1, tb, slots), lambda i: (i, 0, 0), memory_space=pltpu.SMEM)
    tok = pl.BlockSpec((1, tb, d), lambda i: (i // per_b, i % per_b, 0))
    return pl.pallas_call(
        _peer_v_kernel,
        grid=(nb,),
        in_specs=[_table_spec(tab), smem, pl.BlockSpec((tb, 2 * slots), lambda i: (i, 0)), tok,
                  pl.BlockSpec((1, 6, d), lambda i: (i // per_b, 0, 0)),
                  pl.BlockSpec((1, d), lambda i: (0, 0))],
        out_specs=tok,
        out_shape=jax.ShapeDtypeStruct((b, s, d), F32),
        scratch_shapes=_gather_scratch(slots) + [pltpu.VMEM((tb * d // LANES, LANES), F32)],
        compiler_params=_cparams(("arbitrary",)),
        name="peer_v",
    )(tab, eidx, act, x1, mod3, gain)


def _pack_table(w):
    e, d = w.shape
    halves = w.astype(BF16).reshape(e, 2, ROW_WORDS, LANES)
    pairs = jnp.moveaxis(halves, 1, -1)
    words = lax.bitcast_convert_type(pairs, jnp.uint32)
    return lax.bitcast_convert_type(words, I32).reshape(e * ROW_WORDS, LANES)


def _pad_in_weights(w_in):
    cols = []
    src = 0
    for _, real, pad in _SEG:
        blk = w_in[:, src:src + real]
        if pad > real:
            blk = jnp.pad(blk, ((0, 0), (0, pad - real)))
        cols.append(blk)
        src += real
    return jnp.concatenate(cols, axis=1).astype(BF16)


def _pad_lanes(v):
    return jnp.pad(v.reshape(1, -1), ((0, 0), (0, LANES - v.shape[-1])))


def kernel(x, c, positions, w_ada, b_ada, pre_mix_g, post_mix_g, w_in, conv_w, conv_b, dt_bias, a_log,
           d_skip, ssd_norm_g, w_out, pre_ffn_g, post_ffn_g, peer_w_q, peer_sub_keys, peer_u, peer_v):
    b, s, d = x.shape
    depth = w_ada.shape[0]
    inv_freq = ROPE_THETA ** (-jnp.arange(0, HEAD_DIM, 2, dtype=F32) / HEAD_DIM)
    invf = jnp.tile(inv_freq, LANES // inv_freq.shape[0]).reshape(1, LANES)
    pos3 = positions.astype(F32)[..., None]
    for l in range(depth):
        mod3 = _adaln(c, w_ada[l], b_ada[l]).reshape(b, 6, d)
        q, k, v, qi, ki, wi, z, xbc, dt = _inproj(
            x, mod3, pre_mix_g[l].reshape(1, d), pos3, invf, _pad_in_weights(w_in[l]))
        attn = _dsa(qi, ki, wi, q, k, v)
        ssd = _ssd(z, xbc, dt, conv_w[l], conv_b[l].reshape(1, -1), _pad_lanes(dt_bias[l]), _pad_lanes(a_log[l]),
                   _pad_lanes(d_skip[l]), ssd_norm_g[l].reshape(1, -1))
        x1, xlo, xhi, qp = _outproj(attn, ssd, x, mod3, post_mix_g[l].reshape(1, d), pre_ffn_g[l].reshape(1, d),
                                    w_out[l].astype(BF16), peer_w_q[l].astype(BF16))
        eidx, gate = _route(qp, peer_sub_keys[l].astype(BF16))
        act = _peer_u(eidx, xlo, xhi, gate, _pack_table(peer_u[l]))
        x = _peer_v(eidx, act, _pack_table(peer_v[l]), x1, mod3, post_ffn_g[l].reshape(1, d))
    return x
```
